```python
import math
import jax, jax.numpy as jnp
from jax import lax
import numpy as np


D_MODEL = 1024
BATCH = 4
SEQ = 8192
DEPTH = 2

CHUNK = 64
Q_BLOCK = 128
EPS = 1e-6

MLA_HEADS = 6
MLA_Q_RANK = 256
MLA_KV_RANK = 128
MLA_NOPE = 64
MLA_ROPE = 32
MLA_V = 64
ROPE_THETA = 10000.0

DIFF_HEADS = 6
DIFF_QK = 32
DIFF_V = 2 * DIFF_QK

CH_HEADS = 4
CH_HD = 64
CH_LEFT = 8
CH_BAND = (CH_LEFT + 1) * CHUNK
REL_MAX = 256
N_REL = REL_MAX + CHUNK

MLA_WIDTH = MLA_HEADS * MLA_V
DIFF_WIDTH = DIFF_HEADS * DIFF_V
CH_WIDTH = CH_HEADS * CH_HD
MIX_WIDTH = MLA_WIDTH + DIFF_WIDTH + CH_WIDTH

IN_MLA = MLA_Q_RANK + MLA_KV_RANK + MLA_ROPE
IN_DIFF = 2 * DIFF_HEADS * 2 * DIFF_QK + DIFF_HEADS * DIFF_V
IN_CH = 3 * CH_WIDTH
IN_WIDTH = IN_MLA + IN_DIFF + IN_CH

D_FF = 2816
CONV_W = 3

kernel_name = "hybrid_parallel_heads_streaming_encoder"


def rms_norm(x, g):
    xf = x.astype(jnp.float32)
    y = xf * lax.rsqrt(jnp.mean(xf * xf, axis=-1, keepdims=True) + EPS)
    return (y * g.astype(jnp.float32)).astype(x.dtype)


def rope_cos_sin(pos, dim):
    half = dim // 2
    inv = ROPE_THETA ** (-jnp.arange(half, dtype=jnp.float32) / half)
    ang = pos.astype(jnp.float32)[..., None] * inv
    return jnp.cos(ang), jnp.sin(ang)


def apply_rope(x, cos, sin):
    half = x.shape[-1] // 2
    xf = x.astype(jnp.float32)
    x1, x2 = xf[..., :half], xf[..., half:]
    return jnp.concatenate([x1 * cos - x2 * sin, x1 * sin + x2 * cos], axis=-1).astype(x.dtype)


def alibi_slopes(n):
    return 2.0 ** (-8.0 * jnp.arange(1, n + 1, dtype=jnp.float32) / n)


def sweep_query_blocks(block_fn, seq):
    out = lax.map(block_fn, jnp.arange(seq // Q_BLOCK))
    out = jnp.moveaxis(out, 0, 1)
    return out.reshape(out.shape[0], seq, *out.shape[3:])


def chunk_causal_mask(start, seq):
    q_chunk = (start + jnp.arange(Q_BLOCK)) // CHUNK
    k_chunk = jnp.arange(seq) // CHUNK
    return k_chunk[None, :] <= q_chunk[:, None]


def mla_attention(h, pos, q_norm, w_uq, kv_norm, w_ukv):
    B, S, _ = h.shape
    c_q, c_kv, k_r = jnp.split(h, [MLA_Q_RANK, MLA_Q_RANK + MLA_KV_RANK], axis=-1)
    cos, sin = rope_cos_sin(pos, MLA_ROPE)
    q = (rms_norm(c_q, q_norm) @ w_uq).reshape(B, S, MLA_HEADS, MLA_NOPE + MLA_ROPE)
    q_nope = q[..., :MLA_NOPE]
    q_rope = apply_rope(q[..., MLA_NOPE:], cos[:, :, None, :], sin[:, :, None, :])
    kv = (rms_norm(c_kv, kv_norm) @ w_ukv).reshape(B, S, MLA_HEADS, MLA_NOPE + MLA_V)
    k_nope, v = kv[..., :MLA_NOPE], kv[..., MLA_NOPE:]
    k_rope = apply_rope(k_r, cos, sin)
    scale = (MLA_NOPE + MLA_ROPE) ** -0.5

    def block(i):
        start = i * Q_BLOCK
        qn = lax.dynamic_slice_in_dim(q_nope, start, Q_BLOCK, axis=1)
        qr = lax.dynamic_slice_in_dim(q_rope, start, Q_BLOCK, axis=1)
        s = (jnp.einsum('bqhd,bkhd->bhqk', qn, k_nope)
             + jnp.einsum('bqhr,bkr->bhqk', qr, k_rope)).astype(jnp.float32) * scale
        s = jnp.where(chunk_causal_mask(start, S), s, -jnp.inf)
        p = jax.nn.softmax(s, axis=-1)
        return jnp.einsum('bhqk,bkhd->bqhd', p.astype(v.dtype), v)

    return sweep_query_blocks(block, S).reshape(B, S, MLA_WIDTH)


def diff_attention(h, pos, lam, sub_norm, lam_init, slopes):
    B, S, _ = h.shape
    qk_w = DIFF_HEADS * 2 * DIFF_QK
    q, k, v = jnp.split(h, [qk_w, 2 * qk_w], axis=-1)
    q = q.reshape(B, S, DIFF_HEADS, 2, DIFF_QK)
    k = k.reshape(B, S, DIFF_HEADS, 2, DIFF_QK)
    v = v.reshape(B, S, DIFF_HEADS, DIFF_V)
    lamf = lam.astype(jnp.float32)
    lam_full = (jnp.exp(jnp.sum(lamf[0] * lamf[1])) - jnp.exp(jnp.sum(lamf[2] * lamf[3]))
                + lam_init)
    scale = DIFF_QK ** -0.5
    posf = pos.astype(jnp.float32)

    def block(i):
        start = i * Q_BLOCK
        qb = lax.dynamic_slice_in_dim(q, start, Q_BLOCK, axis=1)
        pq = lax.dynamic_slice_in_dim(posf, start, Q_BLOCK, axis=1)
        s = jnp.einsum('bqhmd,bkhmd->bmhqk', qb, k).astype(jnp.float32) * scale
        dist = jnp.abs(pq[:, :, None] - posf[:, None, :])
        s = s - slopes[None, None, :, None, None] * dist[:, None, None]
        s = jnp.where(chunk_causal_mask(start, S), s, -jnp.inf)
        p = jax.nn.softmax(s, axis=-1)
        a = p[:, 0] - lam_full * p[:, 1]
        return jnp.einsum('bhqk,bkhd->bqhd', a.astype(v.dtype), v)

    o = sweep_query_blocks(block, S)
    o = rms_norm(o, sub_norm.reshape(DIFF_HEADS, DIFF_V)) * (1.0 - lam_init)
    return o.reshape(B, S, DIFF_WIDTH)


def chunk_attention(h, rel_bias):
    B, S, _ = h.shape
    nc = S // CHUNK
    q, k, v = [t.reshape(B, nc, CHUNK, CH_HEADS, CH_HD) for t in jnp.split(h, 3, axis=-1)]
    band_idx = jnp.arange(nc)[:, None] + jnp.arange(CH_LEFT + 1)[None, :]

    def band(t):
        tp = jnp.pad(t, ((0, 0), (CH_LEFT, 0), (0, 0), (0, 0), (0, 0)))
        return tp[:, band_idx].reshape(B, nc, CH_BAND, CH_HEADS, CH_HD)

    kb, vb = band(k), band(v)
    s = jnp.einsum('bnqhd,bnkhd->bnhqk', q, kb).astype(jnp.float32) * (CH_HD ** -0.5)
    rel = (CH_LEFT * CHUNK + jnp.arange(CHUNK))[:, None] - jnp.arange(CH_BAND)[None, :]
    rel_idx = jnp.clip(rel, -(CHUNK - 1), REL_MAX) + (CHUNK - 1)
    bias = rel_bias[:, rel_idx].astype(jnp.float32)
    key_chunk = jnp.arange(nc)[:, None] - CH_LEFT + (jnp.arange(CH_BAND) // CHUNK)[None, :]
    valid = key_chunk >= 0
    s = jnp.where(valid[None, :, None, None, :], s + bias[None, None], -jnp.inf)
    p = jax.nn.softmax(s, axis=-1)
    o = jnp.einsum('bnhqk,bnkhd->bnqhd', p.astype(vb.dtype), vb)
    return o.reshape(B, S, CH_WIDTH)


def conv_ffn(x, w_in, conv_w, conv_b, w_out):
    h = x @ w_in
    c = h.shape[-1]
    h = lax.conv_general_dilated(h, conv_w[:, None, :].astype(h.dtype), window_strides=(1,),
                                 padding=[(CONV_W - 1, 0)],
                                 dimension_numbers=('NWC', 'WIO', 'NWC'),
                                 feature_group_count=c) + conv_b
    u, g = jnp.split(h, 2, axis=-1)
    return (jax.nn.silu(g) * u) @ w_out


def setup_inputs(seed: int = 0) -> dict:
    key = jax.random.key(seed)
    ks = jax.random.split(key, 24)
    L, D = DEPTH, D_MODEL
    f32 = jnp.float32

    def w(k, shape, fan_in):
        return jax.random.normal(k, shape, f32) * fan_in ** -0.5

    def gain(k, shape):
        return 1.0 + 0.01 * jax.random.normal(k, shape, f32)

    x = jax.random.normal(ks[0], (BATCH, SEQ, D), f32)
    offsets = jax.random.randint(ks[1], (BATCH, 1), 0, 4096, dtype=jnp.int32)
    positions = offsets + jnp.arange(SEQ, dtype=jnp.int32)[None, :]
    return {
        'x': x,
        'positions': positions,
        'attn_norm': gain(ks[2], (L, D)),
        'w_in': w(ks[3], (L, D, IN_WIDTH), D),
        'mla_q_norm': gain(ks[4], (L, MLA_Q_RANK)),
        'mla_w_uq': w(ks[5], (L, MLA_Q_RANK, MLA_HEADS * (MLA_NOPE + MLA_ROPE)), MLA_Q_RANK),
        'mla_kv_norm': gain(ks[6], (L, MLA_KV_RANK)),
        'mla_w_ukv': w(ks[7], (L, MLA_KV_RANK, MLA_HEADS * (MLA_NOPE + MLA_V)), MLA_KV_RANK),
        'diff_lambda': 0.1 * jax.random.normal(ks[8], (L, 4, DIFF_QK), f32),
        'diff_norm': gain(ks[9], (L, DIFF_WIDTH)),
        'chunk_rel_bias': 0.1 * jax.random.normal(ks[10], (L, CH_HEADS, N_REL), f32),
        'mla_out_norm': gain(ks[11], (L, MLA_WIDTH)),
        'chunk_out_norm': gain(ks[12], (L, CH_WIDTH)),
        'w_out': w(ks[13], (L, MIX_WIDTH, D), MIX_WIDTH),
        'ffn_norm': gain(ks[14], (L, D)),
        'w_ffn_in': w(ks[15], (L, D, 2 * D_FF), D),
        'ffn_conv_w': w(ks[16], (L, CONV_W, 2 * D_FF), CONV_W),
        'ffn_conv_b': 0.01 * jax.random.normal(ks[17], (L, 2 * D_FF), f32),
        'w_ffn_out': w(ks[18], (L, D_FF, D), D_FF),
        'final_norm': gain(ks[19], (D,)),
    }


def reference(x, positions, attn_norm, w_in, mla_q_norm, mla_w_uq, mla_kv_norm, mla_w_ukv,
              diff_lambda, diff_norm, chunk_rel_bias, mla_out_norm, chunk_out_norm, w_out,
              ffn_norm, w_ffn_in, ffn_conv_w, ffn_conv_b, w_ffn_out, final_norm):
    slopes = alibi_slopes(DIFF_HEADS)
    for l in range(DEPTH):
        lam_init = 0.8 - 0.6 * math.exp(-0.3 * l)
        proj = rms_norm(x, attn_norm[l]) @ w_in[l]
        h_mla, h_diff, h_ch = jnp.split(proj, [IN_MLA, IN_MLA + IN_DIFF], axis=-1)
        o_mla = rms_norm(mla_attention(h_mla, positions, mla_q_norm[l], mla_w_uq[l],
                                       mla_kv_norm[l], mla_w_ukv[l]), mla_out_norm[l])
        o_diff = diff_attention(h_diff, positions, diff_lambda[l], diff_norm[l], lam_init, slopes)
        o_ch = rms_norm(chunk_attention(h_ch, chunk_rel_bias[l]), chunk_out_norm[l])
        x = x + jnp.concatenate([o_mla, o_diff, o_ch], axis=-1) @ w_out[l]
        x = x + conv_ffn(rms_norm(x, ffn_norm[l]), w_ffn_in[l], ffn_conv_w[l], ffn_conv_b[l],
                         w_ffn_out[l])
    return rms_norm(x, final_norm)
```

```python
import functools
import math

import jax
import jax.numpy as jnp
from jax import lax
from jax.experimental import pallas as pl
from jax.experimental.pallas import tpu as pltpu

F32 = jnp.float32
BF16 = jnp.bfloat16

D_MODEL = 1024
DEPTH = 2
CHUNK = 64
EPS = 1e-6

MLA_HEADS = 6
MLA_Q_RANK = 256
MLA_KV_RANK = 128
MLA_NOPE = 64
MLA_ROPE = 32
MLA_V = 64
ROPE_THETA = 10000.0

DIFF_HEADS = 6
DIFF_QK = 32
DIFF_V = 64

CH_HEADS = 4
CH_HD = 64
CH_LEFT = 8
CH_BAND = (CH_LEFT + 1) * CHUNK
REL_MAX = 256

MLA_WIDTH = MLA_HEADS * MLA_V
DIFF_WIDTH = DIFF_HEADS * DIFF_V
CH_WIDTH = CH_HEADS * CH_HD
DIFF_QK_WIDTH = DIFF_HEADS * 2 * DIFF_QK

IN_MLA = MLA_Q_RANK + MLA_KV_RANK + MLA_ROPE
D_FF = 2816

LOG2E = math.log2(math.e)

LANES = 128

ROW_TILE = 512
TQ = 256
TK = 256
FF_COLS = 1408
VMEM_LIMIT = 56 * 1024 * 1024

P_CQ = 0
P_CKV = MLA_Q_RANK
P_KR = P_CKV + MLA_KV_RANK
P_KRR = P_KR + LANES
P_DQ = P_KRR + LANES
P_DK = P_DQ + DIFF_QK_WIDTH
P_DV = P_DK + DIFF_QK_WIDTH
P_CQ_ = P_DV + DIFF_WIDTH
P_CK_ = P_CQ_ + CH_WIDTH
P_CV_ = P_CK_ + CH_WIDTH
P_TOTAL = P_CV_ + CH_WIDTH


def _rms(x, g, axis=-1):
    return x * lax.rsqrt(jnp.mean(x * x, axis=axis, keepdims=True) + EPS) * g


def _dot(a, b):
    return jnp.dot(a, b, preferred_element_type=F32)


def _dot_nt(a, b):
    return lax.dot_general(a, b, (((1,), (1,)), ((), ())), preferred_element_type=F32)


def _inproj_kernel(x_ref, pos_ref, g_ref, win_ref, qn_ref, wq_ref, kvn_ref, wkv_ref, inv_ref,
                   qm_ref, km_ref, vmt_ref, qd_ref, kd_ref, vdt_ref, qc_ref, kc_ref, vc_ref):
    tm = x_ref.shape[1]
    xn = _rms(x_ref[0], g_ref[...]).astype(BF16)
    proj = _dot(xn, win_ref[...])

    ang = pos_ref[0] * inv_ref[...]
    cos = jnp.cos(ang)
    sin = jnp.sin(ang)

    cqn = _rms(proj[:, P_CQ:P_CKV], qn_ref[...]).astype(BF16)
    q2 = _dot(cqn, wq_ref[...])
    ckvn = _rms(proj[:, P_CKV:P_KR], kvn_ref[...]).astype(BF16)
    kv2 = _dot(ckvn, wkv_ref[...])

    k_rope = proj[:, P_KR:P_KRR] * cos + proj[:, P_KRR:P_DQ] * sin
    q_scale = (MLA_NOPE + MLA_ROPE) ** -0.5 * LOG2E
    nh = MLA_HEADS * LANES
    for h in range(MLA_HEADS):
        sl = slice(h * LANES, (h + 1) * LANES)
        qh = q2[:, sl] * cos + q2[:, nh + h * LANES: nh + (h + 1) * LANES] * sin
        qm_ref[0, :, sl] = (qh * q_scale).astype(BF16)
        km_ref[0, :, sl] = (kv2[:, sl] + k_rope).astype(BF16)

    vm_t = kv2[:, nh:nh + MLA_WIDTH].T.astype(BF16)
    vd_t = proj[:, P_DV:P_CQ_].T.astype(BF16)
    for h in range(MLA_HEADS):
        for t in range(tm // TK):
            vmt_ref[0, h, t] = vm_t[h * MLA_V:(h + 1) * MLA_V, t * TK:(t + 1) * TK]
            vdt_ref[0, h, t] = vd_t[h * DIFF_V:(h + 1) * DIFF_V, t * TK:(t + 1) * TK]

    qd_ref[0] = (proj[:, P_DQ:P_DK] * (DIFF_QK ** -0.5 * LOG2E)).astype(BF16)
    kd_ref[0] = proj[:, P_DK:P_DV].astype(BF16)
    qc_ref[0] = (proj[:, P_CQ_:P_CK_] * (CH_HD ** -0.5 * LOG2E)).astype(BF16)
    kc_ref[0] = proj[:, P_CK_:P_CV_].astype(BF16)
    vc_ref[0] = proj[:, P_CV_:P_TOTAL].astype(BF16)


def _inproj(x, pos_col, g, win, qn, wq, kvn, wkv, inv):
    B, S, D = x.shape
    tm = min(ROW_TILE, S)
    nt = S // TK
    full = lambda a: pl.BlockSpec(a.shape, lambda b, i: (0,) * a.ndim)
    row = lambda w: pl.BlockSpec((1, tm, w), lambda b, i: (b, i, 0))
    vt_spec = pl.BlockSpec((1, MLA_HEADS, tm // TK, MLA_V, TK), lambda b, i: (b, 0, i, 0, 0))
    out_shape = (
        jax.ShapeDtypeStruct((B, S, MLA_HEADS * LANES), BF16),
        jax.ShapeDtypeStruct((B, S, MLA_HEADS * LANES), BF16),
        jax.ShapeDtypeStruct((B, MLA_HEADS, nt, MLA_V, TK), BF16),
        jax.ShapeDtypeStruct((B, S, DIFF_QK_WIDTH), BF16),
        jax.ShapeDtypeStruct((B, S, DIFF_QK_WIDTH), BF16),
        jax.ShapeDtypeStruct((B, DIFF_HEADS, nt, DIFF_V, TK), BF16),
        jax.ShapeDtypeStruct((B, S, CH_WIDTH), BF16),
        jax.ShapeDtypeStruct((B, S, CH_WIDTH), BF16),
        jax.ShapeDtypeStruct((B, S, CH_WIDTH), BF16),
    )
    return pl.pallas_call(
        _inproj_kernel,
        grid=(B, S // tm),
        in_specs=[row(D), row(1), full(g), full(win), full(qn), full(wq), full(kvn), full(wkv),
                  full(inv)],
        out_specs=(row(MLA_HEADS * LANES), row(MLA_HEADS * LANES), vt_spec,
                   row(DIFF_QK_WIDTH), row(DIFF_QK_WIDTH), vt_spec,
                   row(CH_WIDTH), row(CH_WIDTH), row(CH_WIDTH)),
        out_shape=out_shape,
        compiler_params=pltpu.CompilerParams(
            dimension_semantics=("parallel", "parallel"), vmem_limit_bytes=VMEM_LIMIT),
        name="inproj",
    )(x, pos_col, g, win, qn, wq, kvn, wkv, inv)


def _chunk_causal_tile_mask(n_q):
    r = lax.broadcasted_iota(jnp.int32, (TK, n_q), 0) // CHUNK
    c = (lax.broadcasted_iota(jnp.int32, (TK, n_q), 1) % TQ) // CHUNK
    return r <= c


def _online_softmax_step(s, vt, carry):
    m, l, acc = carry
    m_new = jnp.maximum(m, jnp.max(s, axis=0, keepdims=True))
    alpha = jnp.exp2(m - m_new)
    p = jnp.exp2(s - m_new)
    l = alpha * l + jnp.sum(p, axis=0, keepdims=True)
    acc = alpha * acc + _dot(vt, p.astype(BF16))
    return m_new, l, acc


def _softmax_init(n, dv):
    return (jnp.full((1, n), -jnp.inf, F32), jnp.zeros((1, n), F32), jnp.zeros((dv, n), F32))


def _mla_kernel(q_ref, k_ref, vt_ref, o_ref):
    i = pl.program_id(2)
    q = q_ref[0]

    def scores(j):
        k = k_ref[0, pl.ds(pl.multiple_of(j * TK, TK), TK), :]
        return _dot_nt(k, q)

    def body(j, carry):
        return _online_softmax_step(scores(j), vt_ref[0, 0, j], carry)

    carry = lax.fori_loop(0, i, body, _softmax_init(TQ, MLA_V))
    s = jnp.where(_chunk_causal_tile_mask(TQ), scores(i), -jnp.inf)
    _, l, acc = _online_softmax_step(s, vt_ref[0, 0, i], carry)
    o_ref[0] = acc / l


def _mla_attention(qm, km, vmt):
    B, S, _ = qm.shape
    nt = S // TK
    return pl.pallas_call(
        _mla_kernel,
        grid=(B, MLA_HEADS, S // TQ),
        in_specs=[
            pl.BlockSpec((1, TQ, LANES), lambda b, h, i: (b, i, h)),
            pl.BlockSpec((1, S, LANES), lambda b, h, i: (b, 0, h)),
            pl.BlockSpec((1, 1, nt, MLA_V, TK), lambda b, h, i: (b, h, 0, 0, 0)),
        ],
        out_specs=pl.BlockSpec((1, MLA_V, TQ), lambda b, h, i: (b, h, i)),
        out_shape=jax.ShapeDtypeStruct((B, MLA_WIDTH, S), F32),
        compiler_params=pltpu.CompilerParams(
            dimension_semantics=("parallel", "parallel", "arbitrary"),
            vmem_limit_bytes=VMEM_LIMIT),
        name="mla_attn",
    )(qm, km, vmt)


def _diff_kernel(q_ref, k_ref, vt_ref, posq_ref, posk_ref, slope_ref, lam_ref, g_ref, o_ref,
                 *, lam_init):
    h = pl.program_id(1)
    i = pl.program_id(2)
    q = q_ref[0]
    lane = lax.broadcasted_iota(jnp.int32, q.shape, 1)
    off = (h % 2) * (2 * DIFF_QK)
    zero = jnp.zeros_like(q)
    q0 = jnp.where((lane >= off) & (lane < off + DIFF_QK), q, zero)
    q1 = jnp.where((lane >= off + DIFF_QK) & (lane < off + 2 * DIFF_QK), q, zero)
    qq = jnp.concatenate([q0, q1], axis=0)
    pos_q = posq_ref[0]
    slope = slope_ref[0]

    def scores(j):
        start = pl.multiple_of(j * TK, TK)
        k = k_ref[0, pl.ds(start, TK), :]
        s = _dot_nt(k, qq)
        bias = jnp.abs(posk_ref[0, pl.ds(start, TK), :] - pos_q) * slope
        return s - jnp.concatenate([bias, bias], axis=1)

    def body(j, carry):
        return _online_softmax_step(scores(j), vt_ref[0, 0, j], carry)

    carry = lax.fori_loop(0, i, body, _softmax_init(2 * TQ, DIFF_V))
    s = jnp.where(_chunk_causal_tile_mask(2 * TQ), scores(i), -jnp.inf)
    _, l, acc = _online_softmax_step(s, vt_ref[0, 0, i], carry)
    o = acc / l

    lam = lam_ref[...]
    lam_full = (jnp.exp(jnp.sum(lam[0:1] * lam[1:2], keepdims=True))
                - jnp.exp(jnp.sum(lam[2:3] * lam[3:4], keepdims=True)) + lam_init)
    od = o[:, :TQ] - lam_full * o[:, TQ:]
    o_ref[0] = _rms(od, g_ref[0], axis=0) * (1.0 - lam_init)


def _diff_attention(qd, kd, vdt, pos_row, pos_col, slopes, lam, sub_norm, lam_init):
    B, S, _ = qd.shape
    nt = S // TK
    return pl.pallas_call(
        functools.partial(_diff_kernel, lam_init=lam_init),
        grid=(B, DIFF_HEADS, S // TQ),
        in_specs=[
            pl.BlockSpec((1, TQ, LANES), lambda b, h, i: (b, i, h // 2)),
            pl.BlockSpec((1, S, LANES), lambda b, h, i: (b, 0, h // 2)),
            pl.BlockSpec((1, 1, nt, DIFF_V, TK), lambda b, h, i: (b, h, 0, 0, 0)),
            pl.BlockSpec((1, 1, TQ), lambda b, h, i: (b, 0, i)),
            pl.BlockSpec((1, S, 1), lambda b, h, i: (b, 0, 0)),
            pl.BlockSpec((1, 1, 1), lambda b, h, i: (h, 0, 0)),
            pl.BlockSpec(lam.shape, lambda b, h, i: (0, 0)),
            pl.BlockSpec((1, DIFF_V, 1), lambda b, h, i: (h, 0, 0)),
        ],
        out_specs=pl.BlockSpec((1, DIFF_V, TQ), lambda b, h, i: (b, h, i)),
        out_shape=jax.ShapeDtypeStruct((B, DIFF_WIDTH, S), F32),
        compiler_params=pltpu.CompilerParams(
            dimension_semantics=("parallel", "parallel", "arbitrary"),
            vmem_limit_bytes=VMEM_LIMIT),
        name="diff_attn",
    )(qd, kd, vdt, pos_row, pos_col, slopes, lam, sub_norm)


CH_PAD = CH_LEFT * CHUNK
CH_ROWS = 256


def _chunk_kernel(q_ref, k_ref, v_ref, bias_ref, o_ref):
    i = pl.program_id(1)
    lane = lax.broadcasted_iota(jnp.int32, (CHUNK, LANES), 1)
    key_col = lax.broadcasted_iota(jnp.int32, (CHUNK, CH_BAND), 1)
    heads_per_group = LANES // CH_HD
    for c in range(CH_ROWS // CHUNK):
        start = pl.multiple_of((i * (CH_ROWS // CHUNK) + c) * CHUNK, CHUNK)
        valid = key_col >= CH_PAD - start
        for g in range(CH_WIDTH // LANES):
            gl = slice(g * LANES, (g + 1) * LANES)
            q = q_ref[0, c * CHUNK:(c + 1) * CHUNK, gl]
            k = k_ref[0, pl.ds(start, CH_BAND), gl]
            v = v_ref[0, pl.ds(start, CH_BAND), gl]
            outs = []
            for hh in range(heads_per_group):
                in_head = (lane >= hh * CH_HD) & (lane < (hh + 1) * CH_HD)
                s = _dot_nt(jnp.where(in_head, q, jnp.zeros_like(q)), k)
                s = jnp.where(valid, s + bias_ref[g * heads_per_group + hh], -jnp.inf)
                m = jnp.max(s, axis=1, keepdims=True)
                p = jnp.exp2(s - m)
                l = jnp.sum(p, axis=1, keepdims=True)
                outs.append(_dot(p.astype(BF16), v) / l)
            o_ref[0, c * CHUNK:(c + 1) * CHUNK, gl] = jnp.where(lane < CH_HD, outs[0], outs[1])


def _chunk_attention(qc, kc_pad, vc_pad, bias):
    B, S, _ = qc.shape
    return pl.pallas_call(
        _chunk_kernel,
        grid=(B, S // CH_ROWS),
        in_specs=[
            pl.BlockSpec((1, CH_ROWS, CH_WIDTH), lambda b, i: (b, i, 0)),
            pl.BlockSpec((1, S + CH_PAD, CH_WIDTH), lambda b, i: (b, 0, 0)),
            pl.BlockSpec((1, S + CH_PAD, CH_WIDTH), lambda b, i: (b, 0, 0)),
            pl.BlockSpec(bias.shape, lambda b, i: (0, 0, 0)),
        ],
        out_specs=pl.BlockSpec((1, CH_ROWS, CH_WIDTH), lambda b, i: (b, i, 0)),
        out_shape=jax.ShapeDtypeStruct((B, S, CH_WIDTH), F32),
        compiler_params=pltpu.CompilerParams(
            dimension_semantics=("parallel", "arbitrary"), vmem_limit_bytes=VMEM_LIMIT),
        name="chunk_attn",
    )(qc, kc_pad, vc_pad, bias)


def _outproj_kernel(omt_ref, odt_ref, oc_ref, x_ref, gm_ref, gc_ref, w_ref, o_ref):
    om = _rms(omt_ref[0], gm_ref[...], axis=0).T.astype(BF16)
    od = odt_ref[0].T.astype(BF16)
    oc = _rms(oc_ref[0], gc_ref[...]).astype(BF16)
    mix = jnp.concatenate([om, od, oc], axis=1)
    o_ref[0] = x_ref[0] + _dot(mix, w_ref[...])


def _outproj(omt, odt, oc, x, gm, gc, w):
    B, S, D = x.shape
    tm = min(ROW_TILE, S)
    full = lambda a: pl.BlockSpec(a.shape, lambda b, i: (0,) * a.ndim)
    return pl.pallas_call(
        _outproj_kernel,
        grid=(B, S // tm),
        in_specs=[
            pl.BlockSpec((1, MLA_WIDTH, tm), lambda b, i: (b, 0, i)),
            pl.BlockSpec((1, DIFF_WIDTH, tm), lambda b, i: (b, 0, i)),
            pl.BlockSpec((1, tm, CH_WIDTH), lambda b, i: (b, i, 0)),
            pl.BlockSpec((1, tm, D), lambda b, i: (b, i, 0)),
            full(gm), full(gc), full(w),
        ],
        out_specs=pl.BlockSpec((1, tm, D), lambda b, i: (b, i, 0)),
        out_shape=jax.ShapeDtypeStruct((B, S, D), F32),
        compiler_params=pltpu.CompilerParams(
            dimension_semantics=("parallel", "parallel"), vmem_limit_bytes=VMEM_LIMIT),
        name="outproj",
    )(omt, odt, oc, x, gm, gc, w)


HALO = 8


def _ffn_kernel(x_ref, g_ref, wu_ref, wg_ref, cwu_ref, cwg_ref, cbu_ref, cbg_ref, wo_ref, gf_ref,
                o_ref, xn_ref, acc_ref, hu_ref, hg_ref, carry_ref, *, final_norm):
    i = pl.program_id(1)
    c = pl.program_id(2)
    tm = x_ref.shape[1]

    @pl.when(c == 0)
    def _():
        xn_ref[...] = _rms(x_ref[0], g_ref[...]).astype(BF16)

    def conv(h_ref, w_ref, cw_ref, cb_ref, slot):
        h_ref[HALO:HALO + tm, :] = _dot(xn_ref[...], w_ref[...])

        @pl.when(i == 0)
        def _():
            h_ref[0:HALO, :] = jnp.zeros((HALO, h_ref.shape[1]), F32)

        @pl.when(i > 0)
        def _():
            h_ref[0:HALO, :] = carry_ref[c, slot]

        carry_ref[c, slot] = h_ref[tm:tm + HALO, :]
        cw = cw_ref[...]
        return (cw[0:1] * h_ref[HALO - 2:HALO - 2 + tm, :] + cw[1:2] * h_ref[HALO - 1:HALO - 1 + tm, :]
                + cw[2:3] * h_ref[HALO:HALO + tm, :] + cb_ref[...])

    u = conv(hu_ref, wu_ref, cwu_ref, cbu_ref, 0)
    g = conv(hg_ref, wg_ref, cwg_ref, cbg_ref, 1)
    act = (g / (1.0 + jnp.exp(-g)) * u).astype(BF16)
    part = _dot(act, wo_ref[...])

    @pl.when(c == 0)
    def _():
        acc_ref[...] = part

    @pl.when(c > 0)
    def _():
        acc_ref[...] += part

    @pl.when(c == pl.num_programs(2) - 1)
    def _():
        y = x_ref[0] + acc_ref[...]
        if final_norm:
            y = _rms(y, gf_ref[...])
        o_ref[0] = y


def _ffn(x, g, w_in, conv_w, conv_b, w_out, g_final, final_norm):
    B, S, D = x.shape
    tm = min(ROW_TILE, S)
    nc = D_FF // FF_COLS
    full = lambda a: pl.BlockSpec(a.shape, lambda b, i, c: (0,) * a.ndim)
    ucol = lambda rows: pl.BlockSpec((rows, FF_COLS), lambda b, i, c: (0, c))
    gcol = lambda rows: pl.BlockSpec((rows, FF_COLS), lambda b, i, c: (0, nc + c))
    return pl.pallas_call(
        functools.partial(_ffn_kernel, final_norm=final_norm),
        grid=(B, S // tm, nc),
        in_specs=[
            pl.BlockSpec((1, tm, D), lambda b, i, c: (b, i, 0)),
            full(g),
            ucol(D), gcol(D), ucol(3), gcol(3), ucol(1), gcol(1),
            pl.BlockSpec((FF_COLS, D), lambda b, i, c: (c, 0)),
            full(g_final),
        ],
        out_specs=pl.BlockSpec((1, tm, D), lambda b, i, c: (b, i, 0)),
        out_shape=jax.ShapeDtypeStruct((B, S, D), F32),
        scratch_shapes=[
            pltpu.VMEM((tm, D), BF16),
            pltpu.VMEM((tm, D), F32),
            pltpu.VMEM((HALO + tm, FF_COLS), F32),
            pltpu.VMEM((HALO + tm, FF_COLS), F32),
            pltpu.VMEM((nc, 2, HALO, FF_COLS), F32),
        ],
        compiler_params=pltpu.CompilerParams(
            dimension_semantics=("parallel", "arbitrary", "arbitrary"),
            vmem_limit_bytes=VMEM_LIMIT),
        name="ffn",
    )(x, g, w_in, w_in, conv_w, conv_w, conv_b, conv_b, w_out, g_final)


def _rotate_half_cols(w):
    half = w.shape[1] // 2
    return jnp.concatenate([-w[:, half:], w[:, :half]], axis=1)


def _place(cols, offset, width=LANES):
    return jnp.pad(cols, ((0, 0), (offset, width - offset - cols.shape[1])))


def _layout_w_in(w):
    kr = w[:, MLA_Q_RANK + MLA_KV_RANK:IN_MLA]
    return jnp.concatenate([
        w[:, :MLA_Q_RANK + MLA_KV_RANK],
        _place(kr, MLA_NOPE), _place(_rotate_half_cols(kr), MLA_NOPE),
        w[:, IN_MLA:],
    ], axis=1).astype(BF16)


def _layout_w_uq(w):
    hd = MLA_NOPE + MLA_ROPE
    plain, rot = [], []
    for h in range(MLA_HEADS):
        wh = w[:, h * hd:(h + 1) * hd]
        plain.append(_place(wh, 0))
        rot.append(_place(_rotate_half_cols(wh[:, MLA_NOPE:]), MLA_NOPE))
    return jnp.concatenate(plain + rot, axis=1).astype(BF16)


def _layout_w_ukv(w):
    hd = MLA_NOPE + MLA_V
    ks = [_place(w[:, h * hd:h * hd + MLA_NOPE], 0) for h in range(MLA_HEADS)]
    vs = [w[:, h * hd + MLA_NOPE:(h + 1) * hd] for h in range(MLA_HEADS)]
    return jnp.concatenate(ks + vs, axis=1).astype(BF16)


def _chunk_bias_table(rel_bias):
    rel = (CH_LEFT * CHUNK + jnp.arange(CHUNK))[:, None] - jnp.arange(CH_BAND)[None, :]
    rel_idx = jnp.clip(rel, -(CHUNK - 1), REL_MAX) + (CHUNK - 1)
    return rel_bias[:, rel_idx] * LOG2E


def kernel(x, positions, attn_norm, w_in, mla_q_norm, mla_w_uq, mla_kv_norm, mla_w_ukv, diff_lambda, diff_norm, chunk_rel_bias, mla_out_norm, chunk_out_norm, w_out, ffn_norm, w_ffn_in, ffn_conv_w, ffn_conv_b, w_ffn_out, final_norm):
    B, S, D = x.shape
    posf = positions.astype(F32)
    pos_col = posf[:, :, None]
    pos_row = posf[:, None, :]
    half = MLA_ROPE // 2
    inv = ROPE_THETA ** (-jnp.arange(half, dtype=F32) / half)
    inv_slot = _place(jnp.concatenate([inv, inv])[None, :], MLA_NOPE)
    slopes = 2.0 ** (-8.0 * jnp.arange(1, DIFF_HEADS + 1, dtype=F32) / DIFF_HEADS)
    slopes2 = (slopes * LOG2E).reshape(DIFF_HEADS, 1, 1)
    row = lambda v: v.reshape(1, -1)

    for l in range(DEPTH):
        lam_init = 0.8 - 0.6 * math.exp(-0.3 * l)
        qm, km, vmt, qd, kd, vdt, qc, kc, vc = _inproj(
            x, pos_col, row(attn_norm[l]), _layout_w_in(w_in[l]), row(mla_q_norm[l]),
            _layout_w_uq(mla_w_uq[l]), row(mla_kv_norm[l]), _layout_w_ukv(mla_w_ukv[l]), inv_slot)
        omt = _mla_attention(qm, km, vmt)
        odt = _diff_attention(qd, kd, vdt, pos_row, pos_col, slopes2, diff_lambda[l],
                              diff_norm[l].reshape(DIFF_HEADS, DIFF_V, 1), lam_init)
        pad = ((0, 0), (CH_PAD, 0), (0, 0))
        oc = _chunk_attention(qc, jnp.pad(kc, pad), jnp.pad(vc, pad),
                              _chunk_bias_table(chunk_rel_bias[l]))
        x = _outproj(omt, odt, oc, x, mla_out_norm[l].reshape(-1, 1), row(chunk_out_norm[l]),
                     w_out[l].astype(BF16))
        x = _ffn(x, row(ffn_norm[l]), w_ffn_in[l].astype(BF16), ffn_conv_w[l],
                 row(ffn_conv_b[l]), w_ffn_out[l].astype(BF16), row(final_norm),
                 final_norm=(l == DEPTH - 1))
    return x
```

```python
import functools
import math

import jax
import jax.numpy as jnp
from jax import lax
from jax.experimental import pallas as pl
from jax.experimental.pallas import tpu as pltpu

F32 = jnp.float32
BF16 = jnp.bfloat16

D_MODEL = 1024
DEPTH = 2
CHUNK = 64
EPS = 1e-6

MLA_HEADS = 6
MLA_Q_RANK = 256
MLA_KV_RANK = 128
MLA_NOPE = 64
MLA_ROPE = 32
MLA_V = 64
ROPE_THETA = 10000.0

DIFF_HEADS = 6
DIFF_QK = 32
DIFF_V = 64

CH_HEADS = 4
CH_HD = 64
CH_LEFT = 8
CH_BAND = (CH_LEFT + 1) * CHUNK
REL_MAX = 256

MLA_WIDTH = MLA_HEADS * MLA_V
DIFF_WIDTH = DIFF_HEADS * DIFF_V
CH_WIDTH = CH_HEADS * CH_HD
DIFF_QK_WIDTH = DIFF_HEADS * 2 * DIFF_QK

IN_MLA = MLA_Q_RANK + MLA_KV_RANK + MLA_ROPE
D_FF = 2816

LOG2E = math.log2(math.e)

LANES = 128

ROW_TILE = 512
TQ = 256
TK = 256
FF_COLS = 1408
VT_ROWS = 80
MLA_HEADS_PER_STEP = 6
DIFF_HEADS_PER_STEP = 6
VMEM_LIMIT = 56 * 1024 * 1024

P_CQ = 0
P_CKV = MLA_Q_RANK
P_KR = P_CKV + MLA_KV_RANK
P_KRR = P_KR + LANES
P_DQ = P_KRR + LANES
P_DK = P_DQ + DIFF_QK_WIDTH
P_DV = P_DK + DIFF_QK_WIDTH
P_CQ_ = P_DV + DIFF_WIDTH
P_CK_ = P_CQ_ + CH_WIDTH
P_CV_ = P_CK_ + CH_WIDTH
P_TOTAL = P_CV_ + CH_WIDTH


def _rms(x, g, axis=-1):
    return x * lax.rsqrt(jnp.mean(x * x, axis=axis, keepdims=True) + EPS) * g


def _dot(a, b):
    return jnp.dot(a, b, preferred_element_type=F32)


def _dot_nt(a, b):
    return lax.dot_general(a, b, (((1,), (1,)), ((), ())), preferred_element_type=F32)


def _inproj_kernel(x_ref, pos_ref, g_ref, win_ref, qn_ref, wq_ref, kvn_ref, wkv_ref, inv_ref,
                   qm_ref, km_ref, vmt_ref, qd_ref, kd_ref, vdt_ref, qc_ref, kc_ref, vc_ref):
    tm = x_ref.shape[1]
    xn = _rms(x_ref[0], g_ref[...]).astype(BF16)
    proj = _dot(xn, win_ref[...])

    ang = pos_ref[0] * inv_ref[...]
    cos = jnp.cos(ang)
    sin = jnp.sin(ang)

    cqn = _rms(proj[:, P_CQ:P_CKV], qn_ref[...]).astype(BF16)
    q2 = _dot(cqn, wq_ref[...])
    ckvn = _rms(proj[:, P_CKV:P_KR], kvn_ref[...]).astype(BF16)
    kv2 = _dot(ckvn, wkv_ref[...])

    k_rope = proj[:, P_KR:P_KRR] * cos + proj[:, P_KRR:P_DQ] * sin
    q_scale = (MLA_NOPE + MLA_ROPE) ** -0.5 * LOG2E
    nh = MLA_HEADS * LANES
    for h in range(MLA_HEADS):
        sl = slice(h * LANES, (h + 1) * LANES)
        qh = q2[:, sl] * cos + q2[:, nh + h * LANES: nh + (h + 1) * LANES] * sin
        qm_ref[0, :, sl] = (qh * q_scale).astype(BF16)
        km_ref[0, :, sl] = (kv2[:, sl] + k_rope).astype(BF16)

    vm_t = kv2[:, nh:nh + MLA_WIDTH].T.astype(BF16)
    vd_t = proj[:, P_DV:P_CQ_].T.astype(BF16)
    ones_rows = jnp.where(lax.broadcasted_iota(jnp.int32, (VT_ROWS - MLA_V, TK), 0) == 0,
                          1.0, 0.0).astype(BF16)
    for h in range(MLA_HEADS):
        for t in range(tm // TK):
            vmt_ref[0, h, t, 0:MLA_V, :] = vm_t[h * MLA_V:(h + 1) * MLA_V, t * TK:(t + 1) * TK]
            vdt_ref[0, h, t, 0:DIFF_V, :] = vd_t[h * DIFF_V:(h + 1) * DIFF_V, t * TK:(t + 1) * TK]
            vmt_ref[0, h, t, MLA_V:VT_ROWS, :] = ones_rows
            vdt_ref[0, h, t, DIFF_V:VT_ROWS, :] = ones_rows

    qd_ref[0] = (proj[:, P_DQ:P_DK] * (DIFF_QK ** -0.5 * LOG2E)).astype(BF16)
    kd_ref[0] = proj[:, P_DK:P_DV].astype(BF16)
    qc_ref[0] = (proj[:, P_CQ_:P_CK_] * (CH_HD ** -0.5 * LOG2E)).astype(BF16)
    kc_ref[0] = proj[:, P_CK_:P_CV_].astype(BF16)
    vc_ref[0] = proj[:, P_CV_:P_TOTAL].astype(BF16)


def _inproj(x, pos_col, g, win, qn, wq, kvn, wkv, inv):
    B, S, D = x.shape
    tm = min(ROW_TILE, S)
    nt = S // TK
    full = lambda a: pl.BlockSpec(a.shape, lambda b, i: (0,) * a.ndim)
    row = lambda w: pl.BlockSpec((1, tm, w), lambda b, i: (b, i, 0))
    vt_spec = pl.BlockSpec((1, MLA_HEADS, tm // TK, VT_ROWS, TK), lambda b, i: (b, 0, i, 0, 0))
    out_shape = (
        jax.ShapeDtypeStruct((B, S, MLA_HEADS * LANES), BF16),
        jax.ShapeDtypeStruct((B, S, MLA_HEADS * LANES), BF16),
        jax.ShapeDtypeStruct((B, MLA_HEADS, nt, VT_ROWS, TK), BF16),
        jax.ShapeDtypeStruct((B, S, DIFF_QK_WIDTH), BF16),
        jax.ShapeDtypeStruct((B, S, DIFF_QK_WIDTH), BF16),
        jax.ShapeDtypeStruct((B, DIFF_HEADS, nt, VT_ROWS, TK), BF16),
        jax.ShapeDtypeStruct((B, S, CH_WIDTH), BF16),
        jax.ShapeDtypeStruct((B, S, CH_WIDTH), BF16),
        jax.ShapeDtypeStruct((B, S, CH_WIDTH), BF16),
    )
    return pl.pallas_call(
        _inproj_kernel,
        grid=(B, S // tm),
        in_specs=[row(D), row(1), full(g), full(win), full(qn), full(wq), full(kvn), full(wkv),
                  full(inv)],
        out_specs=(row(MLA_HEADS * LANES), row(MLA_HEADS * LANES), vt_spec,
                   row(DIFF_QK_WIDTH), row(DIFF_QK_WIDTH), vt_spec,
                   row(CH_WIDTH), row(CH_WIDTH), row(CH_WIDTH)),
        out_shape=out_shape,
        compiler_params=pltpu.CompilerParams(
            dimension_semantics=("parallel", "parallel"), vmem_limit_bytes=VMEM_LIMIT),
        name="inproj",
    )(x, pos_col, g, win, qn, wq, kvn, wkv, inv)


def _chunk_causal_tile_mask(n_q):
    r = lax.broadcasted_iota(jnp.int32, (TK, n_q), 0) // CHUNK
    c = (lax.broadcasted_iota(jnp.int32, (TK, n_q), 1) % TQ) // CHUNK
    return r <= c


def _softmax_tile_update(s, vt, m_ref, acc_ref, h):
    m = m_ref[h]
    m_new = jnp.maximum(m, jnp.max(s, axis=0, keepdims=True))
    p = jnp.exp2(s - m_new).astype(BF16)
    acc_ref[h] = jnp.exp2(m - m_new) * acc_ref[h] + _dot(vt, p)
    m_ref[h] = m_new


def _softmax_state_init(m_ref, acc_ref):
    m_ref[...] = jnp.full(m_ref.shape, -jnp.inf, F32)
    acc_ref[...] = jnp.zeros(acc_ref.shape, F32)


def _softmax_result(acc_ref, h, dv):
    acc = acc_ref[h]
    return acc[0:dv] / acc[dv:dv + 1]


def _pipelined_key_sweep(i, scores, consume):
    scores(0, 0)

    def body(jj, carry):
        j = 2 * jj
        scores(j + 1, 1)
        consume(j, 0)
        scores(j + 2, 0)
        consume(j + 1, 1)
        return carry

    lax.fori_loop(0, i // 2, body, 0)

    @pl.when(i % 2 == 1)
    def _():
        scores(i, 1)
        consume(i - 1, 0)
        consume(i, 1, masked=True)

    @pl.when(i % 2 == 0)
    def _():
        consume(i, 0, masked=True)


def _softmax_scratch(heads, n):
    return [pltpu.VMEM((heads, 1, n), F32), pltpu.VMEM((heads, VT_ROWS, n), F32)]


def _mla_kernel(q_ref, k_ref, vt_ref, o_ref, s_ref, m_ref, acc_ref):
    i = pl.program_id(2)
    heads = m_ref.shape[0]
    _softmax_state_init(m_ref, acc_ref)

    def scores(j, slot):
        start = pl.multiple_of(j * TK, TK)
        for h in range(heads):
            sl = slice(h * LANES, (h + 1) * LANES)
            s_ref[slot, h] = _dot_nt(k_ref[0, pl.ds(start, TK), sl], q_ref[0, :, sl])

    def consume(j, slot, masked=False):
        for h in range(heads):
            s = s_ref[slot, h]
            if masked:
                s = jnp.where(_chunk_causal_tile_mask(TQ), s, -jnp.inf)
            _softmax_tile_update(s, vt_ref[0, h, j], m_ref, acc_ref, h)

    _pipelined_key_sweep(i, scores, consume)
    for h in range(heads):
        o_ref[0, h * MLA_V:(h + 1) * MLA_V, :] = _softmax_result(acc_ref, h, MLA_V)


def _mla_attention(qm, km, vmt):
    B, S, _ = qm.shape
    nt = S // TK
    hb = MLA_HEADS_PER_STEP
    return pl.pallas_call(
        _mla_kernel,
        grid=(B, MLA_HEADS // hb, S // TQ),
        in_specs=[
            pl.BlockSpec((1, TQ, hb * LANES), lambda b, h, i: (b, i, h)),
            pl.BlockSpec((1, S, hb * LANES), lambda b, h, i: (b, 0, h)),
            pl.BlockSpec((1, hb, nt, VT_ROWS, TK), lambda b, h, i: (b, h, 0, 0, 0)),
        ],
        out_specs=pl.BlockSpec((1, hb * MLA_V, TQ), lambda b, h, i: (b, h, i)),
        out_shape=jax.ShapeDtypeStruct((B, MLA_WIDTH, S), F32),
        scratch_shapes=[pltpu.VMEM((2, hb, TK, TQ), F32)] + _softmax_scratch(hb, TQ),
        compiler_params=pltpu.CompilerParams(
            dimension_semantics=("parallel", "parallel", "arbitrary"),
            vmem_limit_bytes=VMEM_LIMIT),
        name="mla_attn",
    )(qm, km, vmt)


def _diff_kernel(q_ref, k_ref, vt_ref, posq_ref, posk_ref, slope_ref, lam_ref, g_ref, o_ref,
                 qq_ref, s_ref, m_ref, acc_ref, *, lam_init):
    i = pl.program_id(2)
    heads = m_ref.shape[0]
    heads_per_group = LANES // (2 * DIFF_QK)
    _softmax_state_init(m_ref, acc_ref)

    lane = lax.broadcasted_iota(jnp.int32, (TQ, LANES), 1)
    for h in range(heads):
        q = q_ref[0, :, (h // heads_per_group) * LANES:(h // heads_per_group + 1) * LANES]
        off = (h % heads_per_group) * 2 * DIFF_QK
        for mp in range(2):
            lo = off + mp * DIFF_QK
            qq_ref[h, mp * TQ:(mp + 1) * TQ, :] = jnp.where(
                (lane >= lo) & (lane < lo + DIFF_QK), q, jnp.zeros_like(q))
    pos_q = posq_ref[0]

    def scores(j, slot):
        start = pl.multiple_of(j * TK, TK)
        for h in range(heads):
            g = h // heads_per_group
            s_ref[slot, h] = _dot_nt(k_ref[0, pl.ds(start, TK), g * LANES:(g + 1) * LANES],
                                     qq_ref[h])

    def consume(j, slot, masked=False):
        start = pl.multiple_of(j * TK, TK)
        dist = jnp.abs(posk_ref[0, pl.ds(start, TK), :] - pos_q)
        for h in range(heads):
            bias = dist * slope_ref[h]
            s = s_ref[slot, h] - jnp.concatenate([bias, bias], axis=1)
            if masked:
                s = jnp.where(_chunk_causal_tile_mask(2 * TQ), s, -jnp.inf)
            _softmax_tile_update(s, vt_ref[0, h, j], m_ref, acc_ref, h)

    _pipelined_key_sweep(i, scores, consume)

    lam = lam_ref[...]
    lam_full = (jnp.exp(jnp.sum(lam[0:1] * lam[1:2], keepdims=True))
                - jnp.exp(jnp.sum(lam[2:3] * lam[3:4], keepdims=True)) + lam_init)
    for h in range(heads):
        o = _softmax_result(acc_ref, h, DIFF_V)
        od = o[:, :TQ] - lam_full * o[:, TQ:]
        o_ref[0, h * DIFF_V:(h + 1) * DIFF_V, :] = _rms(od, g_ref[h], axis=0) * (1.0 - lam_init)


def _diff_attention(qd, kd, vdt, pos_row, pos_col, slopes, lam, sub_norm, lam_init):
    B, S, _ = qd.shape
    nt = S // TK
    hb = DIFF_HEADS_PER_STEP
    qk_w = hb * 2 * DIFF_QK
    return pl.pallas_call(
        functools.partial(_diff_kernel, lam_init=lam_init),
        grid=(B, DIFF_HEADS // hb, S // TQ),
        in_specs=[
            pl.BlockSpec((1, TQ, qk_w), lambda b, h, i: (b, i, h)),
            pl.BlockSpec((1, S, qk_w), lambda b, h, i: (b, 0, h)),
            pl.BlockSpec((1, hb, nt, VT_ROWS, TK), lambda b, h, i: (b, h, 0, 0, 0)),
            pl.BlockSpec((1, 1, TQ), lambda b, h, i: (b, 0, i)),
            pl.BlockSpec((1, S, 1), lambda b, h, i: (b, 0, 0)),
            pl.BlockSpec((hb, 1, 1), lambda b, h, i: (h, 0, 0)),
            pl.BlockSpec(lam.shape, lambda b, h, i: (0, 0)),
            pl.BlockSpec((hb, DIFF_V, 1), lambda b, h, i: (h, 0, 0)),
        ],
        out_specs=pl.BlockSpec((1, hb * DIFF_V, TQ), lambda b, h, i: (b, h, i)),
        out_shape=jax.ShapeDtypeStruct((B, DIFF_WIDTH, S), F32),
        scratch_shapes=[pltpu.VMEM((hb, 2 * TQ, LANES), BF16),
                        pltpu.VMEM((2, hb, TK, 2 * TQ), F32)] + _softmax_scratch(hb, 2 * TQ),
        compiler_params=pltpu.CompilerParams(
            dimension_semantics=("parallel", "parallel", "arbitrary"),
            vmem_limit_bytes=VMEM_LIMIT),
        name="diff_attn",
    )(qd, kd, vdt, pos_row, pos_col, slopes, lam, sub_norm)


CH_PAD = CH_LEFT * CHUNK
CH_ROWS = 256


def _chunk_kernel(q_ref, k_ref, v_ref, bias_ref, o_ref):
    i = pl.program_id(1)
    lane = lax.broadcasted_iota(jnp.int32, (CHUNK, LANES), 1)
    key_col = lax.broadcasted_iota(jnp.int32, (CHUNK, CH_BAND), 1)
    heads_per_group = LANES // CH_HD
    for c in range(CH_ROWS // CHUNK):
        start = pl.multiple_of((i * (CH_ROWS // CHUNK) + c) * CHUNK, CHUNK)
        valid = key_col >= CH_PAD - start
        for g in range(CH_WIDTH // LANES):
            gl = slice(g * LANES, (g + 1) * LANES)
            q = q_ref[0, c * CHUNK:(c + 1) * CHUNK, gl]
            k = k_ref[0, pl.ds(start, CH_BAND), gl]
            v = v_ref[0, pl.ds(start, CH_BAND), gl]
            outs = []
            for hh in range(heads_per_group):
                in_head = (lane >= hh * CH_HD) & (lane < (hh + 1) * CH_HD)
                s = _dot_nt(jnp.where(in_head, q, jnp.zeros_like(q)), k)
                s = jnp.where(valid, s + bias_ref[g * heads_per_group + hh], -jnp.inf)
                m = jnp.max(s, axis=1, keepdims=True)
                p = jnp.exp2(s - m)
                l = jnp.sum(p, axis=1, keepdims=True)
                outs.append(_dot(p.astype(BF16), v) / l)
            o_ref[0, c * CHUNK:(c + 1) * CHUNK, gl] = jnp.where(lane < CH_HD, outs[0], outs[1])


def _chunk_attention(qc, kc_pad, vc_pad, bias):
    B, S, _ = qc.shape
    return pl.pallas_call(
        _chunk_kernel,
        grid=(B, S // CH_ROWS),
        in_specs=[
            pl.BlockSpec((1, CH_ROWS, CH_WIDTH), lambda b, i: (b, i, 0)),
            pl.BlockSpec((1, S + CH_PAD, CH_WIDTH), lambda b, i: (b, 0, 0)),
            pl.BlockSpec((1, S + CH_PAD, CH_WIDTH), lambda b, i: (b, 0, 0)),
            pl.BlockSpec(bias.shape, lambda b, i: (0, 0, 0)),
        ],
        out_specs=pl.BlockSpec((1, CH_ROWS, CH_WIDTH), lambda b, i: (b, i, 0)),
        out_shape=jax.ShapeDtypeStruct((B, S, CH_WIDTH), F32),
        compiler_params=pltpu.CompilerParams(
            dimension_semantics=("parallel", "arbitrary"), vmem_limit_bytes=VMEM_LIMIT),
        name="chunk_attn",
    )(qc, kc_pad, vc_pad, bias)


def _outproj_kernel(omt_ref, odt_ref, oc_ref, x_ref, gm_ref, gc_ref, w_ref, o_ref):
    om = _rms(omt_ref[0], gm_ref[...], axis=0).T.astype(BF16)
    od = odt_ref[0].T.astype(BF16)
    oc = _rms(oc_ref[0], gc_ref[...]).astype(BF16)
    mix = jnp.concatenate([om, od, oc], axis=1)
    o_ref[0] = x_ref[0] + _dot(mix, w_ref[...])


def _outproj(omt, odt, oc, x, gm, gc, w):
    B, S, D = x.shape
    tm = min(ROW_TILE, S)
    full = lambda a: pl.BlockSpec(a.shape, lambda b, i: (0,) * a.ndim)
    return pl.pallas_call(
        _outproj_kernel,
        grid=(B, S // tm),
        in_specs=[
            pl.BlockSpec((1, MLA_WIDTH, tm), lambda b, i: (b, 0, i)),
            pl.BlockSpec((1, DIFF_WIDTH, tm), lambda b, i: (b, 0, i)),
            pl.BlockSpec((1, tm, CH_WIDTH), lambda b, i: (b, i, 0)),
            pl.BlockSpec((1, tm, D), lambda b, i: (b, i, 0)),
            full(gm), full(gc), full(w),
        ],
        out_specs=pl.BlockSpec((1, tm, D), lambda b, i: (b, i, 0)),
        out_shape=jax.ShapeDtypeStruct((B, S, D), F32),
        compiler_params=pltpu.CompilerParams(
            dimension_semantics=("parallel", "parallel"), vmem_limit_bytes=VMEM_LIMIT),
        name="outproj",
    )(omt, odt, oc, x, gm, gc, w)


HALO = 8


def _ffn_kernel(x_ref, g_ref, wu_ref, wg_ref, cwu_ref, cwg_ref, cbu_ref, cbg_ref, wo_ref, gf_ref,
                o_ref, xn_ref, acc_ref, hu_ref, hg_ref, carry_ref, *, final_norm):
    i = pl.program_id(1)
    c = pl.program_id(2)
    tm = x_ref.shape[1]

    @pl.when(c == 0)
    def _():
        xn_ref[...] = _rms(x_ref[0], g_ref[...]).astype(BF16)

    def conv(h_ref, w_ref, cw_ref, cb_ref, slot):
        h_ref[HALO:HALO + tm, :] = _dot(xn_ref[...], w_ref[...])

        @pl.when(i == 0)
        def _():
            h_ref[0:HALO, :] = jnp.zeros((HALO, h_ref.shape[1]), F32)

        @pl.when(i > 0)
        def _():
            h_ref[0:HALO, :] = carry_ref[c, slot]

        carry_ref[c, slot] = h_ref[tm:tm + HALO, :]
        cw = cw_ref[...]
        return (cw[0:1] * h_ref[HALO - 2:HALO - 2 + tm, :] + cw[1:2] * h_ref[HALO - 1:HALO - 1 + tm, :]
                + cw[2:3] * h_ref[HALO:HALO + tm, :] + cb_ref[...])

    u = conv(hu_ref, wu_ref, cwu_ref, cbu_ref, 0)
    g = conv(hg_ref, wg_ref, cwg_ref, cbg_ref, 1)
    act = (g / (1.0 + jnp.exp(-g)) * u).astype(BF16)
    part = _dot(act, wo_ref[...])

    @pl.when(c == 0)
    def _():
        acc_ref[...] = part

    @pl.when(c > 0)
    def _():
        acc_ref[...] += part

    @pl.when(c == pl.num_programs(2) - 1)
    def _():
        y = x_ref[0] + acc_ref[...]
        if final_norm:
            y = _rms(y, gf_ref[...])
        o_ref[0] = y


def _ffn(x, g, w_in, conv_w, conv_b, w_out, g_final, final_norm):
    B, S, D = x.shape
    tm = min(ROW_TILE, S)
    nc = D_FF // FF_COLS
    full = lambda a: pl.BlockSpec(a.shape, lambda b, i, c: (0,) * a.ndim)
    ucol = lambda rows: pl.BlockSpec((rows, FF_COLS), lambda b, i, c: (0, c))
    gcol = lambda rows: pl.BlockSpec((rows, FF_COLS), lambda b, i, c: (0, nc + c))
    return pl.pallas_call(
        functools.partial(_ffn_kernel, final_norm=final_norm),
        grid=(B, S // tm, nc),
        in_specs=[
            pl.BlockSpec((1, tm, D), lambda b, i, c: (b, i, 0)),
            full(g),
            ucol(D), gcol(D), ucol(3), gcol(3), ucol(1), gcol(1),
            pl.BlockSpec((FF_COLS, D), lambda b, i, c: (c, 0)),
            full(g_final),
        ],
        out_specs=pl.BlockSpec((1, tm, D), lambda b, i, c: (b, i, 0)),
        out_shape=jax.ShapeDtypeStruct((B, S, D), F32),
        scratch_shapes=[
            pltpu.VMEM((tm, D), BF16),
            pltpu.VMEM((tm, D), F32),
            pltpu.VMEM((HALO + tm, FF_COLS), F32),
            pltpu.VMEM((HALO + tm, FF_COLS), F32),
            pltpu.VMEM((nc, 2, HALO, FF_COLS), F32),
        ],
        compiler_params=pltpu.CompilerParams(
            dimension_semantics=("parallel", "arbitrary", "arbitrary"),
            vmem_limit_bytes=VMEM_LIMIT),
        name="ffn",
    )(x, g, w_in, w_in, conv_w, conv_w, conv_b, conv_b, w_out, g_final)


def _rotate_half_cols(w):
    half = w.shape[1] // 2
    return jnp.concatenate([-w[:, half:], w[:, :half]], axis=1)


def _place(cols, offset, width=LANES):
    return jnp.pad(cols, ((0, 0), (offset, width - offset - cols.shape[1])))


def _layout_w_in(w):
    kr = w[:, MLA_Q_RANK + MLA_KV_RANK:IN_MLA]
    return jnp.concatenate([
        w[:, :MLA_Q_RANK + MLA_KV_RANK],
        _place(kr, MLA_NOPE), _place(_rotate_half_cols(kr), MLA_NOPE),
        w[:, IN_MLA:],
    ], axis=1).astype(BF16)


def _layout_w_uq(w):
    hd = MLA_NOPE + MLA_ROPE
    plain, rot = [], []
    for h in range(MLA_HEADS):
        wh = w[:, h * hd:(h + 1) * hd]
        plain.append(_place(wh, 0))
        rot.append(_place(_rotate_half_cols(wh[:, MLA_NOPE:]), MLA_NOPE))
    return jnp.concatenate(plain + rot, axis=1).astype(BF16)


def _layout_w_ukv(w):
    hd = MLA_NOPE + MLA_V
    ks = [_place(w[:, h * hd:h * hd + MLA_NOPE], 0) for h in range(MLA_HEADS)]
    vs = [w[:, h * hd + MLA_NOPE:(h + 1) * hd] for h in range(MLA_HEADS)]
    return jnp.concatenate(ks + vs, axis=1).astype(BF16)


def _chunk_bias_table(rel_bias):
    rel = (CH_LEFT * CHUNK + jnp.arange(CHUNK))[:, None] - jnp.arange(CH_BAND)[None, :]
    rel_idx = jnp.clip(rel, -(CHUNK - 1), REL_MAX) + (CHUNK - 1)
    return rel_bias[:, rel_idx] * LOG2E


def kernel(x, positions, attn_norm, w_in, mla_q_norm, mla_w_uq, mla_kv_norm, mla_w_ukv, diff_lambda, diff_norm, chunk_rel_bias, mla_out_norm, chunk_out_norm, w_out, ffn_norm, w_ffn_in, ffn_conv_w, ffn_conv_b, w_ffn_out, final_norm):
    B, S, D = x.shape
    posf = positions.astype(F32)
    pos_col = posf[:, :, None]
    pos_row = posf[:, None, :]
    half = MLA_ROPE // 2
    inv = ROPE_THETA ** (-jnp.arange(half, dtype=F32) / half)
    inv_slot = _place(jnp.concatenate([inv, inv])[None, :], MLA_NOPE)
    slopes = 2.0 ** (-8.0 * jnp.arange(1, DIFF_HEADS + 1, dtype=F32) / DIFF_HEADS)
    slopes2 = (slopes * LOG2E).reshape(DIFF_HEADS, 1, 1)
    row = lambda v: v.reshape(1, -1)

    for l in range(DEPTH):
        lam_init = 0.8 - 0.6 * math.exp(-0.3 * l)
        qm, km, vmt, qd, kd, vdt, qc, kc, vc = _inproj(
            x, pos_col, row(attn_norm[l]), _layout_w_in(w_in[l]), row(mla_q_norm[l]),
            _layout_w_uq(mla_w_uq[l]), row(mla_kv_norm[l]), _layout_w_ukv(mla_w_ukv[l]), inv_slot)
        omt = _mla_attention(qm, km, vmt)
        odt = _diff_attention(qd, kd, vdt, pos_row, pos_col, slopes2, diff_lambda[l],
                              diff_norm[l].reshape(DIFF_HEADS, DIFF_V, 1), lam_init)
        pad = ((0, 0), (CH_PAD, 0), (0, 0))
        oc = _chunk_attention(qc, jnp.pad(kc, pad), jnp.pad(vc, pad),
                              _chunk_bias_table(chunk_rel_bias[l]))
        x = _outproj(omt, odt, oc, x, mla_out_norm[l].reshape(-1, 1), row(chunk_out_norm[l]),
                     w_out[l].astype(BF16))
        x = _ffn(x, row(ffn_norm[l]), w_ffn_in[l].astype(BF16), ffn_conv_w[l],
                 row(ffn_conv_b[l]), w_ffn_out[l].astype(BF16), row(final_norm),
                 final_norm=(l == DEPTH - 1))
    return x
```

```python
import functools
import math

import jax
import jax.numpy as jnp
from jax import lax
from jax.experimental import pallas as pl
from jax.experimental.pallas import tpu as pltpu

F32 = jnp.float32
BF16 = jnp.bfloat16

D_MODEL = 1024
DEPTH = 2
CHUNK = 64
EPS = 1e-6

MLA_HEADS = 6
MLA_Q_RANK = 256
MLA_KV_RANK = 128
MLA_NOPE = 64
MLA_ROPE = 32
MLA_V = 64
ROPE_THETA = 10000.0

DIFF_HEADS = 6
DIFF_QK = 32
DIFF_V = 64

CH_HEADS = 4
CH_HD = 64
CH_LEFT = 8
CH_BAND = (CH_LEFT + 1) * CHUNK
REL_MAX = 256

MLA_WIDTH = MLA_HEADS * MLA_V
DIFF_WIDTH = DIFF_HEADS * DIFF_V
CH_WIDTH = CH_HEADS * CH_HD
DIFF_QK_WIDTH = DIFF_HEADS * 2 * DIFF_QK

IN_MLA = MLA_Q_RANK + MLA_KV_RANK + MLA_ROPE
D_FF = 2816

LOG2E = math.log2(math.e)

LANES = 128

ROW_TILE = 512
TQ = 256
TK = 256
FF_CHUNKS = ((0, 1024), (1024, 1024), (2048, 768))
VT_ROWS = 80
MLA_HEADS_PER_STEP = 6
DIFF_HEADS_PER_STEP = 6
VMEM_LIMIT = 56 * 1024 * 1024

P_CQ = 0
P_CKV = MLA_Q_RANK
P_KR = P_CKV + MLA_KV_RANK
P_KRR = P_KR + LANES
P_DQ = P_KRR + LANES
P_DK = P_DQ + DIFF_QK_WIDTH
P_DV = P_DK + DIFF_QK_WIDTH
P_CQ_ = P_DV + DIFF_WIDTH
P_CK_ = P_CQ_ + CH_WIDTH
P_CV_ = P_CK_ + CH_WIDTH
P_TOTAL = P_CV_ + CH_WIDTH


def _rms(x, g, axis=-1):
    return x * lax.rsqrt(jnp.mean(x * x, axis=axis, keepdims=True) + EPS) * g


def _dot(a, b):
    return jnp.dot(a, b, preferred_element_type=F32)


def _dot_nt(a, b):
    return lax.dot_general(a, b, (((1,), (1,)), ((), ())), preferred_element_type=F32)


def _inproj_kernel(x_ref, pos_ref, g_ref, win_ref, qn_ref, wq_ref, kvn_ref, wkv_ref, inv_ref,
                   qm_ref, km_ref, vmt_ref, qd_ref, kd_ref, vdt_ref, qc_ref, kc_ref, vc_ref):
    tm = x_ref.shape[1]
    xn = _rms(x_ref[0], g_ref[...]).astype(BF16)
    proj = _dot(xn, win_ref[...])

    ang = pos_ref[0] * inv_ref[...]
    cos = jnp.cos(ang)
    sin = jnp.sin(ang)

    cqn = _rms(proj[:, P_CQ:P_CKV], qn_ref[...]).astype(BF16)
    q2 = _dot(cqn, wq_ref[...])
    ckvn = _rms(proj[:, P_CKV:P_KR], kvn_ref[...]).astype(BF16)
    kv2 = _dot(ckvn, wkv_ref[...])

    k_rope = proj[:, P_KR:P_KRR] * cos + proj[:, P_KRR:P_DQ] * sin
    q_scale = (MLA_NOPE + MLA_ROPE) ** -0.5 * LOG2E
    nh = MLA_HEADS * LANES
    for h in range(MLA_HEADS):
        sl = slice(h * LANES, (h + 1) * LANES)
        qh = q2[:, sl] * cos + q2[:, nh + h * LANES: nh + (h + 1) * LANES] * sin
        qm_ref[0, :, sl] = (qh * q_scale).astype(BF16)
        km_ref[0, :, sl] = (kv2[:, sl] + k_rope).astype(BF16)

    vm_t = kv2[:, nh:nh + MLA_WIDTH].T.astype(BF16)
    vd_t = proj[:, P_DV:P_CQ_].T.astype(BF16)
    ones_rows = jnp.where(lax.broadcasted_iota(jnp.int32, (VT_ROWS - MLA_V, TK), 0) == 0,
                          1.0, 0.0).astype(BF16)
    for h in range(MLA_HEADS):
        for t in range(tm // TK):
            vmt_ref[0, h, t, 0:MLA_V, :] = vm_t[h * MLA_V:(h + 1) * MLA_V, t * TK:(t + 1) * TK]
            vdt_ref[0, h, t, 0:DIFF_V, :] = vd_t[h * DIFF_V:(h + 1) * DIFF_V, t * TK:(t + 1) * TK]
            vmt_ref[0, h, t, MLA_V:VT_ROWS, :] = ones_rows
            vdt_ref[0, h, t, DIFF_V:VT_ROWS, :] = ones_rows

    qd_ref[0] = (proj[:, P_DQ:P_DK] * (DIFF_QK ** -0.5 * LOG2E)).astype(BF16)
    kd_ref[0] = proj[:, P_DK:P_DV].astype(BF16)
    qc_ref[0] = (proj[:, P_CQ_:P_CK_] * (CH_HD ** -0.5 * LOG2E)).astype(BF16)
    kc_ref[0] = proj[:, P_CK_:P_CV_].astype(BF16)
    vc_ref[0] = proj[:, P_CV_:P_TOTAL].astype(BF16)


def _inproj(x, pos_col, g, win, qn, wq, kvn, wkv, inv):
    B, S, D = x.shape
    tm = min(ROW_TILE, S)
    nt = S // TK
    full = lambda a: pl.BlockSpec(a.shape, lambda b, i: (0,) * a.ndim)
    row = lambda w: pl.BlockSpec((1, tm, w), lambda b, i: (b, i, 0))
    vt_spec = pl.BlockSpec((1, MLA_HEADS, tm // TK, VT_ROWS, TK), lambda b, i: (b, 0, i, 0, 0))
    out_shape = (
        jax.ShapeDtypeStruct((B, S, MLA_HEADS * LANES), BF16),
        jax.ShapeDtypeStruct((B, S, MLA_HEADS * LANES), BF16),
        jax.ShapeDtypeStruct((B, MLA_HEADS, nt, VT_ROWS, TK), BF16),
        jax.ShapeDtypeStruct((B, S, DIFF_QK_WIDTH), BF16),
        jax.ShapeDtypeStruct((B, S, DIFF_QK_WIDTH), BF16),
        jax.ShapeDtypeStruct((B, DIFF_HEADS, nt, VT_ROWS, TK), BF16),
        jax.ShapeDtypeStruct((B, S, CH_WIDTH), BF16),
        jax.ShapeDtypeStruct((B, S, CH_WIDTH), BF16),
        jax.ShapeDtypeStruct((B, S, CH_WIDTH), BF16),
    )
    return pl.pallas_call(
        _inproj_kernel,
        grid=(B, S // tm),
        in_specs=[row(D), row(1), full(g), full(win), full(qn), full(wq), full(kvn), full(wkv),
                  full(inv)],
        out_specs=(row(MLA_HEADS * LANES), row(MLA_HEADS * LANES), vt_spec,
                   row(DIFF_QK_WIDTH), row(DIFF_QK_WIDTH), vt_spec,
                   row(CH_WIDTH), row(CH_WIDTH), row(CH_WIDTH)),
        out_shape=out_shape,
        compiler_params=pltpu.CompilerParams(
            dimension_semantics=("parallel", "parallel"), vmem_limit_bytes=VMEM_LIMIT),
        name="inproj",
    )(x, pos_col, g, win, qn, wq, kvn, wkv, inv)


def _chunk_causal_tile_mask(n_q):
    r = lax.broadcasted_iota(jnp.int32, (TK, n_q), 0) // CHUNK
    c = (lax.broadcasted_iota(jnp.int32, (TK, n_q), 1) % TQ) // CHUNK
    return r <= c


def _store_scores(s, s_ref, smax_ref, slot, h, masked):
    if masked:
        s = jnp.where(_chunk_causal_tile_mask(s.shape[1]), s, -jnp.inf)
    s_ref[slot, h] = s
    smax_ref[slot, h] = jnp.max(s, axis=0, keepdims=True)


def _softmax_tile_update(vt, s_ref, smax_ref, m_ref, acc_ref, slot, h):
    m = m_ref[h]
    m_new = jnp.maximum(m, smax_ref[slot, h])
    p = jnp.exp2(s_ref[slot, h] - m_new).astype(BF16)
    acc_ref[h] = jnp.exp2(m - m_new) * acc_ref[h] + _dot(vt, p)
    m_ref[h] = m_new


def _softmax_state_init(m_ref, acc_ref):
    m_ref[...] = jnp.full(m_ref.shape, -jnp.inf, F32)
    acc_ref[...] = jnp.zeros(acc_ref.shape, F32)


def _softmax_result(acc_ref, h, dv):
    acc = acc_ref[h]
    return acc[0:dv] / acc[dv:dv + 1]


def _pipelined_key_sweep(i, scores, consume):
    @pl.when(i == 0)
    def _():
        scores(0, 0, True)
        consume(0, 0)

    @pl.when(i > 0)
    def _():
        scores(0, 0, False)

        def body(jj, carry):
            j = 2 * jj
            scores(j + 1, 1, False)
            consume(j, 0)
            scores(j + 2, 0, False)
            consume(j + 1, 1)
            return carry

        lax.fori_loop(0, (i - 1) // 2, body, 0)

        @pl.when(i % 2 == 1)
        def _():
            scores(i, 1, True)
            consume(i - 1, 0)
            consume(i, 1)

        @pl.when(i % 2 == 0)
        def _():
            scores(i - 1, 1, False)
            consume(i - 2, 0)
            scores(i, 0, True)
            consume(i - 1, 1)
            consume(i, 0)


def _softmax_scratch(heads, n):
    return [pltpu.VMEM((2, heads, TK, n), F32), pltpu.VMEM((2, heads, 1, n), F32),
            pltpu.VMEM((heads, 1, n), F32), pltpu.VMEM((heads, VT_ROWS, n), F32)]


def _mla_kernel(q_ref, k_ref, vt_ref, o_ref, s_ref, smax_ref, m_ref, acc_ref):
    i = pl.program_id(2)
    heads = m_ref.shape[0]
    _softmax_state_init(m_ref, acc_ref)

    def scores(j, slot, masked):
        start = pl.multiple_of(j * TK, TK)
        for h in range(heads):
            sl = slice(h * LANES, (h + 1) * LANES)
            s = _dot_nt(k_ref[0, pl.ds(start, TK), sl], q_ref[0, :, sl])
            _store_scores(s, s_ref, smax_ref, slot, h, masked)

    def consume(j, slot):
        for h in range(heads):
            _softmax_tile_update(vt_ref[0, h, j], s_ref, smax_ref, m_ref, acc_ref, slot, h)

    _pipelined_key_sweep(i, scores, consume)
    for h in range(heads):
        o_ref[0, h * MLA_V:(h + 1) * MLA_V, :] = _softmax_result(acc_ref, h, MLA_V)


def _mla_attention(qm, km, vmt):
    B, S, _ = qm.shape
    nt = S // TK
    hb = MLA_HEADS_PER_STEP
    return pl.pallas_call(
        _mla_kernel,
        grid=(B, MLA_HEADS // hb, S // TQ),
        in_specs=[
            pl.BlockSpec((1, TQ, hb * LANES), lambda b, h, i: (b, i, h)),
            pl.BlockSpec((1, S, hb * LANES), lambda b, h, i: (b, 0, h)),
            pl.BlockSpec((1, hb, nt, VT_ROWS, TK), lambda b, h, i: (b, h, 0, 0, 0)),
        ],
        out_specs=pl.BlockSpec((1, hb * MLA_V, TQ), lambda b, h, i: (b, h, i)),
        out_shape=jax.ShapeDtypeStruct((B, MLA_WIDTH, S), F32),
        scratch_shapes=_softmax_scratch(hb, TQ),
        compiler_params=pltpu.CompilerParams(
            dimension_semantics=("parallel", "parallel", "arbitrary"),
            vmem_limit_bytes=VMEM_LIMIT),
        name="mla_attn",
    )(qm, km, vmt)


def _diff_kernel(q_ref, k_ref, vt_ref, posq_ref, posk_ref, slope_ref, lam_ref, g_ref, o_ref,
                 qq_ref, s_ref, smax_ref, m_ref, acc_ref, *, lam_init):
    i = pl.program_id(2)
    heads = m_ref.shape[0]
    heads_per_group = LANES // (2 * DIFF_QK)
    _softmax_state_init(m_ref, acc_ref)

    lane = lax.broadcasted_iota(jnp.int32, (TQ, LANES), 1)
    for h in range(heads):
        q = q_ref[0, :, (h // heads_per_group) * LANES:(h // heads_per_group + 1) * LANES]
        off = (h % heads_per_group) * 2 * DIFF_QK
        for mp in range(2):
            lo = off + mp * DIFF_QK
            qq_ref[h, mp * TQ:(mp + 1) * TQ, :] = jnp.where(
                (lane >= lo) & (lane < lo + DIFF_QK), q, jnp.zeros_like(q))
    pos_q = posq_ref[0]

    def scores(j, slot, masked):
        start = pl.multiple_of(j * TK, TK)
        dist = jnp.abs(posk_ref[0, pl.ds(start, TK), :] - pos_q)
        for h in range(heads):
            g = h // heads_per_group
            s = _dot_nt(k_ref[0, pl.ds(start, TK), g * LANES:(g + 1) * LANES], qq_ref[h])
            bias = dist * slope_ref[h]
            s = s - jnp.concatenate([bias, bias], axis=1)
            _store_scores(s, s_ref, smax_ref, slot, h, masked)

    def consume(j, slot):
        for h in range(heads):
            _softmax_tile_update(vt_ref[0, h, j], s_ref, smax_ref, m_ref, acc_ref, slot, h)

    _pipelined_key_sweep(i, scores, consume)

    lam = lam_ref[...]
    lam_full = (jnp.exp(jnp.sum(lam[0:1] * lam[1:2], keepdims=True))
                - jnp.exp(jnp.sum(lam[2:3] * lam[3:4], keepdims=True)) + lam_init)
    for h in range(heads):
        o = _softmax_result(acc_ref, h, DIFF_V)
        od = o[:, :TQ] - lam_full * o[:, TQ:]
        o_ref[0, h * DIFF_V:(h + 1) * DIFF_V, :] = _rms(od, g_ref[h], axis=0) * (1.0 - lam_init)


def _diff_attention(qd, kd, vdt, pos_row, pos_col, slopes, lam, sub_norm, lam_init):
    B, S, _ = qd.shape
    nt = S // TK
    hb = DIFF_HEADS_PER_STEP
    qk_w = hb * 2 * DIFF_QK
    return pl.pallas_call(
        functools.partial(_diff_kernel, lam_init=lam_init),
        grid=(B, DIFF_HEADS // hb, S // TQ),
        in_specs=[
            pl.BlockSpec((1, TQ, qk_w), lambda b, h, i: (b, i, h)),
            pl.BlockSpec((1, S, qk_w), lambda b, h, i: (b, 0, h)),
            pl.BlockSpec((1, hb, nt, VT_ROWS, TK), lambda b, h, i: (b, h, 0, 0, 0)),
            pl.BlockSpec((1, 1, TQ), lambda b, h, i: (b, 0, i)),
            pl.BlockSpec((1, S, 1), lambda b, h, i: (b, 0, 0)),
            pl.BlockSpec((hb, 1, 1), lambda b, h, i: (h, 0, 0)),
            pl.BlockSpec(lam.shape, lambda b, h, i: (0, 0)),
            pl.BlockSpec((hb, DIFF_V, 1), lambda b, h, i: (h, 0, 0)),
        ],
        out_specs=pl.BlockSpec((1, hb * DIFF_V, TQ), lambda b, h, i: (b, h, i)),
        out_shape=jax.ShapeDtypeStruct((B, DIFF_WIDTH, S), F32),
        scratch_shapes=[pltpu.VMEM((hb, 2 * TQ, LANES), BF16)] + _softmax_scratch(hb, 2 * TQ),
        compiler_params=pltpu.CompilerParams(
            dimension_semantics=("parallel", "parallel", "arbitrary"),
            vmem_limit_bytes=VMEM_LIMIT),
        name="diff_attn",
    )(qd, kd, vdt, pos_row, pos_col, slopes, lam, sub_norm)


CH_PAD = CH_LEFT * CHUNK
CH_ROWS = 256


def _chunk_kernel(q_ref, k_ref, v_ref, diag_ref, o_ref, bias_ref):
    i = pl.program_id(1)
    heads_per_group = LANES // CH_HD
    groups = CH_WIDTH // LANES
    chunks = CH_ROWS // CHUNK

    @pl.when(i == 0)
    def _():
        for h in range(CH_HEADS):
            rows = jnp.broadcast_to(diag_ref[h:h + 1, :], (CHUNK, diag_ref.shape[1]))
            bias_ref[h] = pltpu.roll(rows, 0, 1, stride=1, stride_axis=0)[:, :CH_BAND]

    lane = lax.broadcasted_iota(jnp.int32, (CHUNK, LANES), 1)
    key_col = lax.broadcasted_iota(jnp.int32, (CHUNK, CH_BAND), 1)
    blocks = [(c, g) for c in range(chunks) for g in range(groups)]
    starts = [pl.multiple_of((i * chunks + c) * CHUNK, CHUNK) for c in range(chunks)]

    scores = []
    for c, g in blocks:
        gl = slice(g * LANES, (g + 1) * LANES)
        q = q_ref[0, c * CHUNK:(c + 1) * CHUNK, gl]
        qq = jnp.concatenate(
            [jnp.where((lane >= hh * CH_HD) & (lane < (hh + 1) * CH_HD), q, jnp.zeros_like(q))
             for hh in range(heads_per_group)], axis=0)
        scores.append(_dot_nt(qq, k_ref[0, pl.ds(starts[c], CH_BAND), gl]))

    probs, denoms = [], []
    for (c, g), s in zip(blocks, scores):
        valid = key_col >= CH_PAD - starts[c]
        s = jnp.concatenate(
            [jnp.where(valid, s[hh * CHUNK:(hh + 1) * CHUNK] + bias_ref[g * heads_per_group + hh],
                       -jnp.inf) for hh in range(heads_per_group)], axis=0)
        p = jnp.exp2(s - jnp.max(s, axis=1, keepdims=True))
        denoms.append(jnp.sum(p, axis=1, keepdims=True))
        probs.append(p.astype(BF16))

    for (c, g), p, l in zip(blocks, probs, denoms):
        gl = slice(g * LANES, (g + 1) * LANES)
        o = _dot(p, v_ref[0, pl.ds(starts[c], CH_BAND), gl]) / l
        o_ref[0, c * CHUNK:(c + 1) * CHUNK, gl] = jnp.where(lane < CH_HD, o[:CHUNK], o[CHUNK:])


def _chunk_attention(qc, kc_pad, vc_pad, diag):
    B, S, _ = qc.shape
    return pl.pallas_call(
        _chunk_kernel,
        grid=(B, S // CH_ROWS),
        in_specs=[
            pl.BlockSpec((1, CH_ROWS, CH_WIDTH), lambda b, i: (b, i, 0)),
            pl.BlockSpec((1, S + CH_PAD, CH_WIDTH), lambda b, i: (b, 0, 0)),
            pl.BlockSpec((1, S + CH_PAD, CH_WIDTH), lambda b, i: (b, 0, 0)),
            pl.BlockSpec(diag.shape, lambda b, i: (0, 0)),
        ],
        out_specs=pl.BlockSpec((1, CH_ROWS, CH_WIDTH), lambda b, i: (b, i, 0)),
        out_shape=jax.ShapeDtypeStruct((B, S, CH_WIDTH), F32),
        scratch_shapes=[pltpu.VMEM((CH_HEADS, CHUNK, CH_BAND), F32)],
        compiler_params=pltpu.CompilerParams(
            dimension_semantics=("parallel", "arbitrary"), vmem_limit_bytes=VMEM_LIMIT),
        name="chunk_attn",
    )(qc, kc_pad, vc_pad, diag)


def _outproj_kernel(omt_ref, odt_ref, oc_ref, x_ref, gm_ref, gc_ref, w_ref, o_ref):
    om = _rms(omt_ref[0], gm_ref[...], axis=0).T.astype(BF16)
    od = odt_ref[0].T.astype(BF16)
    oc = _rms(oc_ref[0], gc_ref[...]).astype(BF16)
    mix = jnp.concatenate([om, od, oc], axis=1)
    o_ref[0] = x_ref[0] + _dot(mix, w_ref[...])


def _outproj(omt, odt, oc, x, gm, gc, w):
    B, S, D = x.shape
    tm = min(ROW_TILE, S)
    full = lambda a: pl.BlockSpec(a.shape, lambda b, i: (0,) * a.ndim)
    return pl.pallas_call(
        _outproj_kernel,
        grid=(B, S // tm),
        in_specs=[
            pl.BlockSpec((1, MLA_WIDTH, tm), lambda b, i: (b, 0, i)),
            pl.BlockSpec((1, DIFF_WIDTH, tm), lambda b, i: (b, 0, i)),
            pl.BlockSpec((1, tm, CH_WIDTH), lambda b, i: (b, i, 0)),
            pl.BlockSpec((1, tm, D), lambda b, i: (b, i, 0)),
            full(gm), full(gc), full(w),
        ],
        out_specs=pl.BlockSpec((1, tm, D), lambda b, i: (b, i, 0)),
        out_shape=jax.ShapeDtypeStruct((B, S, D), F32),
        compiler_params=pltpu.CompilerParams(
            dimension_semantics=("parallel", "parallel"), vmem_limit_bytes=VMEM_LIMIT),
        name="outproj",
    )(omt, odt, oc, x, gm, gc, w)


HALO = 8


def _ffn_kernel(x_ref, g_ref, wi_ref, cw_ref, cb_ref, wo_ref, gf_ref, o_ref, h_ref, carry_ref,
                *, final_norm):
    i = pl.program_id(1)
    tm = x_ref.shape[1]

    @pl.when(i == 0)
    def _():
        carry_ref[...] = jnp.zeros(carry_ref.shape, F32)

    x = x_ref[0]
    xn = _rms(x, g_ref[...]).astype(BF16)

    def project(c):
        lo, w = FF_CHUNKS[c]
        for half in range(2):
            col = half * D_FF + lo
            h_ref[c % 2, half, HALO:HALO + tm, 0:w] = _dot(xn, wi_ref[:, col:col + w])

    def gate(c):
        lo, w = FF_CHUNKS[c]
        halves = []
        for half in range(2):
            col = half * D_FF + lo
            h = h_ref.at[c % 2, half]
            h[0:HALO, 0:w] = carry_ref[half, :, lo:lo + w]
            carry_ref[half, :, lo:lo + w] = h[tm:tm + HALO, 0:w]
            cw = cw_ref[:, col:col + w]
            halves.append(cw[0:1] * h[HALO - 2:HALO - 2 + tm, 0:w]
                          + cw[1:2] * h[HALO - 1:HALO - 1 + tm, 0:w]
                          + cw[2:3] * h[HALO:HALO + tm, 0:w] + cb_ref[:, col:col + w])
        u, g = halves
        return (g / (1.0 + jnp.exp(-g)) * u).astype(BF16)

    project(0)
    y = x
    for c, (lo, w) in enumerate(FF_CHUNKS):
        if c + 1 < len(FF_CHUNKS):
            project(c + 1)
        y = y + _dot(gate(c), wo_ref[lo:lo + w, :])
    if final_norm:
        y = _rms(y, gf_ref[...])
    o_ref[0] = y


def _ffn(x, g, w_in, conv_w, conv_b, w_out, g_final, final_norm):
    B, S, D = x.shape
    tm = min(ROW_TILE, S)
    resident = lambda a: pl.BlockSpec(a.shape, lambda b, i: (0,) * a.ndim,
                                      pipeline_mode=pl.Buffered(1))
    chunk_w = max(w for _, w in FF_CHUNKS)
    return pl.pallas_call(
        functools.partial(_ffn_kernel, final_norm=final_norm),
        grid=(B, S // tm),
        in_specs=[
            pl.BlockSpec((1, tm, D), lambda b, i: (b, i, 0)),
            resident(g), resident(w_in), resident(conv_w), resident(conv_b), resident(w_out),
            resident(g_final),
        ],
        out_specs=pl.BlockSpec((1, tm, D), lambda b, i: (b, i, 0)),
        out_shape=jax.ShapeDtypeStruct((B, S, D), F32),
        scratch_shapes=[
            pltpu.VMEM((2, 2, HALO + tm, chunk_w), F32),
            pltpu.VMEM((2, HALO, D_FF), F32),
        ],
        compiler_params=pltpu.CompilerParams(
            dimension_semantics=("parallel", "arbitrary"), vmem_limit_bytes=VMEM_LIMIT),
        name="ffn",
    )(x, g, w_in, conv_w, conv_b, w_out, g_final)


def _rotate_half_cols(w):
    half = w.shape[1] // 2
    return jnp.concatenate([-w[:, half:], w[:, :half]], axis=1)


def _place(cols, offset, width=LANES):
    return jnp.pad(cols, ((0, 0), (offset, width - offset - cols.shape[1])))


def _layout_w_in(w):
    kr = w[:, MLA_Q_RANK + MLA_KV_RANK:IN_MLA]
    return jnp.concatenate([
        w[:, :MLA_Q_RANK + MLA_KV_RANK],
        _place(kr, MLA_NOPE), _place(_rotate_half_cols(kr), MLA_NOPE),
        w[:, IN_MLA:],
    ], axis=1).astype(BF16)


def _layout_w_uq(w):
    hd = MLA_NOPE + MLA_ROPE
    plain, rot = [], []
    for h in range(MLA_HEADS):
        wh = w[:, h * hd:(h + 1) * hd]
        plain.append(_place(wh, 0))
        rot.append(_place(_rotate_half_cols(wh[:, MLA_NOPE:]), MLA_NOPE))
    return jnp.concatenate(plain + rot, axis=1).astype(BF16)


def _layout_w_ukv(w):
    hd = MLA_NOPE + MLA_V
    ks = [_place(w[:, h * hd:h * hd + MLA_NOPE], 0) for h in range(MLA_HEADS)]
    vs = [w[:, h * hd + MLA_NOPE:(h + 1) * hd] for h in range(MLA_HEADS)]
    return jnp.concatenate(ks + vs, axis=1).astype(BF16)


def _chunk_bias_diagonals(rel_bias):
    n_rel = rel_bias.shape[1]
    far = jnp.broadcast_to(rel_bias[:, n_rel - 1:], (rel_bias.shape[0], REL_MAX))
    wrap = far[:, :CHUNK]
    return jnp.concatenate([far, rel_bias[:, ::-1], wrap], axis=1) * LOG2E


def kernel(x, positions, attn_norm, w_in, mla_q_norm, mla_w_uq, mla_kv_norm, mla_w_ukv, diff_lambda, diff_norm, chunk_rel_bias, mla_out_norm, chunk_out_norm, w_out, ffn_norm, w_ffn_in, ffn_conv_w, ffn_conv_b, w_ffn_out, final_norm):
    B, S, D = x.shape
    posf = positions.astype(F32)
    pos_col = posf[:, :, None]
    pos_row = posf[:, None, :]
    half = MLA_ROPE // 2
    inv = ROPE_THETA ** (-jnp.arange(half, dtype=F32) / half)
    inv_slot = _place(jnp.concatenate([inv, inv])[None, :], MLA_NOPE)
    slopes = 2.0 ** (-8.0 * jnp.arange(1, DIFF_HEADS + 1, dtype=F32) / DIFF_HEADS)
    slopes2 = (slopes * LOG2E).reshape(DIFF_HEADS, 1, 1)
    row = lambda v: v.reshape(1, -1)

    for l in range(DEPTH):
        lam_init = 0.8 - 0.6 * math.exp(-0.3 * l)
        qm, km, vmt, qd, kd, vdt, qc, kc, vc = _inproj(
            x, pos_col, row(attn_norm[l]), _layout_w_in(w_in[l]), row(mla_q_norm[l]),
            _layout_w_uq(mla_w_uq[l]), row(mla_kv_norm[l]), _layout_w_ukv(mla_w_ukv[l]), inv_slot)
        omt = _mla_attention(qm, km, vmt)
        odt = _diff_attention(qd, kd, vdt, pos_row, pos_col, slopes2, diff_lambda[l],
                              diff_norm[l].reshape(DIFF_HEADS, DIFF_V, 1), lam_init)
        pad = ((0, 0), (CH_PAD, 0), (0, 0))
        oc = _chunk_attention(qc, jnp.pad(kc, pad), jnp.pad(vc, pad),
                              _chunk_bias_diagonals(chunk_rel_bias[l]))
        x = _outproj(omt, odt, oc, x, mla_out_norm[l].reshape(-1, 1), row(chunk_out_norm[l]),
                     w_out[l].astype(BF16))
        x = _ffn(x, row(ffn_norm[l]), w_ffn_in[l].astype(BF16), ffn_conv_w[l],
                 row(ffn_conv_b[l]), w_ffn_out[l].astype(BF16), row(final_norm),
                 final_norm=(l == DEPTH - 1))
    return x
```

```python
import functools
import math

import jax
import jax.numpy as jnp
from jax import lax
from jax.experimental import pallas as pl
from jax.experimental.pallas import tpu as pltpu

F32 = jnp.float32
BF16 = jnp.bfloat16

D_MODEL = 1024
DEPTH = 2
CHUNK = 64
EPS = 1e-6

MLA_HEADS = 6
MLA_Q_RANK = 256
MLA_KV_RANK = 128
MLA_NOPE = 64
MLA_ROPE = 32
MLA_V = 64
ROPE_THETA = 10000.0

DIFF_HEADS = 6
DIFF_QK = 32
DIFF_V = 64

CH_HEADS = 4
CH_HD = 64
CH_LEFT = 8
CH_BAND = (CH_LEFT + 1) * CHUNK
REL_MAX = 256

MLA_WIDTH = MLA_HEADS * MLA_V
DIFF_WIDTH = DIFF_HEADS * DIFF_V
CH_WIDTH = CH_HEADS * CH_HD
DIFF_QK_WIDTH = DIFF_HEADS * 2 * DIFF_QK

IN_MLA = MLA_Q_RANK + MLA_KV_RANK + MLA_ROPE
D_FF = 2816

LOG2E = math.log2(math.e)

LANES = 128

ROW_TILE = 512
TQ = 256
TK = 256
FF_CHUNKS = ((0, 1024), (1024, 1024), (2048, 768))
VT_ROWS = 80
MLA_HEADS_PER_STEP = 6
DIFF_HEADS_PER_STEP = 6
POS_COL0 = 2 * DIFF_QK
VMEM_LIMIT = 56 * 1024 * 1024

P_CQ = 0
P_CKV = MLA_Q_RANK
P_KR = P_CKV + MLA_KV_RANK
P_KRR = P_KR + LANES
P_DQ = P_KRR + LANES
P_DK = P_DQ + DIFF_QK_WIDTH
P_DV = P_DK + DIFF_QK_WIDTH
P_CQ_ = P_DV + DIFF_WIDTH
P_CK_ = P_CQ_ + CH_WIDTH
P_CV_ = P_CK_ + CH_WIDTH
P_TOTAL = P_CV_ + CH_WIDTH


def _rms(x, g, axis=-1):
    return x * lax.rsqrt(jnp.mean(x * x, axis=axis, keepdims=True) + EPS) * g


def _dot(a, b):
    return jnp.dot(a, b, preferred_element_type=F32)


def _dot_nt(a, b):
    return lax.dot_general(a, b, (((1,), (1,)), ((), ())), preferred_element_type=F32)


def _inproj_kernel(x_ref, pos_ref, rel_ref, g_ref, win_ref, qn_ref, wq_ref, kvn_ref, wkv_ref,
                   inv_ref, qcols_ref, qm_ref, km_ref, vmt_ref, qd_ref, kd_ref, vdt_ref, qc_ref, kc_ref, vc_ref):
    tm = x_ref.shape[1]
    xn = _rms(x_ref[0], g_ref[...]).astype(BF16)
    proj = _dot(xn, win_ref[...])

    ang = pos_ref[0] * inv_ref[...]
    cos = jnp.cos(ang)
    sin = jnp.sin(ang)

    cqn = _rms(proj[:, P_CQ:P_CKV], qn_ref[...]).astype(BF16)
    q2 = _dot(cqn, wq_ref[...])
    ckvn = _rms(proj[:, P_CKV:P_KR], kvn_ref[...]).astype(BF16)
    kv2 = _dot(ckvn, wkv_ref[...])

    k_rope = proj[:, P_KR:P_KRR] * cos + proj[:, P_KRR:P_DQ] * sin
    q_scale = (MLA_NOPE + MLA_ROPE) ** -0.5 * LOG2E
    nh = MLA_HEADS * LANES
    q_heads = []
    for h in range(MLA_HEADS):
        sl = slice(h * LANES, (h + 1) * LANES)
        qh = q2[:, sl] * cos + q2[:, nh + h * LANES: nh + (h + 1) * LANES] * sin
        q_heads.append(qh * q_scale)
        km_ref[0, :, sl] = (kv2[:, sl] + k_rope).astype(BF16)
    qm_ref[0] = jnp.concatenate(q_heads, axis=1).T.astype(BF16)

    vm_t = kv2[:, nh:nh + MLA_WIDTH].T.astype(BF16)
    vd_t = proj[:, P_DV:P_CQ_].T.astype(BF16)
    ones_rows = jnp.where(lax.broadcasted_iota(jnp.int32, (VT_ROWS - MLA_V, TK), 0) == 0,
                          1.0, 0.0).astype(BF16)
    for h in range(MLA_HEADS):
        for t in range(tm // TK):
            vmt_ref[0, h, t, 0:MLA_V, :] = vm_t[h * MLA_V:(h + 1) * MLA_V, t * TK:(t + 1) * TK]
            vdt_ref[0, h, t, 0:DIFF_V, :] = vd_t[h * DIFF_V:(h + 1) * DIFF_V, t * TK:(t + 1) * TK]
            vmt_ref[0, h, t, MLA_V:VT_ROWS, :] = ones_rows
            vdt_ref[0, h, t, DIFF_V:VT_ROWS, :] = ones_rows

    lane = lax.broadcasted_iota(jnp.int32, (1, LANES), 1)
    rel = rel_ref[0]
    digits = [(rel >> 16).astype(F32), ((rel >> 8) & 255).astype(F32), (rel & 255).astype(F32)]
    in_cols = (lane >= POS_COL0) & (lane < POS_COL0 + 9)
    k_cols = jnp.where(in_cols, jnp.where(lane < POS_COL0 + 3, digits[0],
                                          jnp.where(lane < POS_COL0 + 6, digits[1], digits[2])), 0.0)
    qd = proj[:, P_DQ:P_DK] * (DIFF_QK ** -0.5 * LOG2E)
    kd = proj[:, P_DK:P_DV]
    heads_per_group = LANES // (2 * DIFF_QK)
    q_heads = []
    for h in range(DIFF_HEADS):
        gl = slice((h // heads_per_group) * LANES, (h // heads_per_group + 1) * LANES)
        sl = slice(h * LANES, (h + 1) * LANES)
        qh, kh = qd[:, gl], kd[:, gl]
        if h % heads_per_group:
            qh, kh = pltpu.roll(qh, 2 * DIFF_QK, 1), pltpu.roll(kh, 2 * DIFF_QK, 1)
        q_heads.append(jnp.where(lane < 2 * DIFF_QK, qh, qcols_ref[:, sl]))
        kd_ref[0, :, sl] = jnp.where(lane < 2 * DIFF_QK, kh, k_cols).astype(BF16)
    qd_ref[0] = jnp.concatenate(q_heads, axis=1).T.astype(BF16)
    qc_ref[0] = (proj[:, P_CQ_:P_CK_] * (CH_HD ** -0.5 * LOG2E)).astype(BF16)
    kc_ref[0] = proj[:, P_CK_:P_CV_].astype(BF16)
    vc_ref[0] = proj[:, P_CV_:P_TOTAL].astype(BF16)


def _inproj(x, pos_col, rel_col, g, win, qn, wq, kvn, wkv, inv, qcols):
    B, S, D = x.shape
    tm = min(ROW_TILE, S)
    nt = S // TK
    full = lambda a: pl.BlockSpec(a.shape, lambda b, i: (0,) * a.ndim)
    row = lambda w: pl.BlockSpec((1, tm, w), lambda b, i: (b, i, 0))
    col = lambda w: pl.BlockSpec((1, w, tm), lambda b, i: (b, 0, i))
    vt_spec =pl.BlockSpec((1, MLA_HEADS, tm // TK, VT_ROWS, TK), lambda b, i: (b, 0, i, 0, 0))
    out_shape = (
        jax.ShapeDtypeStruct((B, MLA_HEADS * LANES, S), BF16),
        jax.ShapeDtypeStruct((B, S, MLA_HEADS * LANES), BF16),
        jax.ShapeDtypeStruct((B, MLA_HEADS, nt, VT_ROWS, TK), BF16),
        jax.ShapeDtypeStruct((B, DIFF_HEADS * LANES, S), BF16),
        jax.ShapeDtypeStruct((B, S, DIFF_HEADS * LANES), BF16),
        jax.ShapeDtypeStruct((B, DIFF_HEADS, nt, VT_ROWS, TK), BF16),
        jax.ShapeDtypeStruct((B, S, CH_WIDTH), BF16),
        jax.ShapeDtypeStruct((B, S, CH_WIDTH), BF16),
        jax.ShapeDtypeStruct((B, S, CH_WIDTH), BF16),
    )
    return pl.pallas_call(
        _inproj_kernel,
        grid=(B, S // tm),
        in_specs=[row(D), row(1), row(1), full(g), full(win), full(qn), full(wq), full(kvn),
                  full(wkv), full(inv), full(qcols)],
        out_specs=(col(MLA_HEADS * LANES), row(MLA_HEADS * LANES), vt_spec,
                   col(DIFF_HEADS * LANES), row(DIFF_HEADS * LANES), vt_spec,
                   row(CH_WIDTH), row(CH_WIDTH), row(CH_WIDTH)),
        out_shape=out_shape,
        compiler_params=pltpu.CompilerParams(
            dimension_semantics=("parallel", "parallel"), vmem_limit_bytes=VMEM_LIMIT),
        name="inproj",
    )(x, pos_col, rel_col, g, win, qn, wq, kvn, wkv, inv, qcols)


def _chunk_causal_tile_mask(n_q):
    r = lax.broadcasted_iota(jnp.int32, (TK, n_q), 0) // CHUNK
    c = (lax.broadcasted_iota(jnp.int32, (TK, n_q), 1) % TQ) // CHUNK
    return r <= c


def _store_scores(s, s_ref, smax_ref, slot, h, masked):
    if masked:
        s = jnp.where(_chunk_causal_tile_mask(s.shape[1]), s, -jnp.inf)
    s_ref[slot, h] = s
    smax_ref[slot, h] = jnp.max(s, axis=0, keepdims=True)


def _softmax_tile_update(vt, s_ref, smax_ref, m_ref, acc_ref, slot, h):
    m = m_ref[h]
    m_new = jnp.maximum(m, smax_ref[slot, h])
    p = jnp.exp2(s_ref[slot, h] - m_new).astype(BF16)
    acc_ref[h] = jnp.exp2(m - m_new) * acc_ref[h] + _dot(vt, p)
    m_ref[h] = m_new


def _softmax_state_init(m_ref, acc_ref):
    m_ref[...] = jnp.full(m_ref.shape, -jnp.inf, F32)
    acc_ref[...] = jnp.zeros(acc_ref.shape, F32)


def _softmax_result(acc_ref, h, dv):
    acc = acc_ref[h]
    return acc[0:dv] / acc[dv:dv + 1]


def _pipelined_key_sweep(i, scores, consume):
    @pl.when(i == 0)
    def _():
        scores(0, 0, True)
        consume(0, 0)

    @pl.when(i > 0)
    def _():
        scores(0, 0, False)

        def body(jj, carry):
            j = 2 * jj
            scores(j + 1, 1, False)
            consume(j, 0)
            scores(j + 2, 0, False)
            consume(j + 1, 1)
            return carry

        lax.fori_loop(0, (i - 1) // 2, body, 0)

        @pl.when(i % 2 == 1)
        def _():
            scores(i, 1, True)
            consume(i - 1, 0)
            consume(i, 1)

        @pl.when(i % 2 == 0)
        def _():
            scores(i - 1, 1, False)
            consume(i - 2, 0)
            scores(i, 0, True)
            consume(i - 1, 1)
            consume(i, 0)


def _softmax_scratch(heads, n):
    return [pltpu.VMEM((2, heads, TK, n), F32), pltpu.VMEM((2, heads, 1, n), F32),
            pltpu.VMEM((heads, 1, n), F32), pltpu.VMEM((heads, VT_ROWS, n), F32)]


def _mla_kernel(q_ref, k_ref, vt_ref, o_ref, s_ref, smax_ref, m_ref, acc_ref):
    i = pl.program_id(2)
    heads = m_ref.shape[0]
    _softmax_state_init(m_ref, acc_ref)

    def scores(j, slot, masked):
        start = pl.multiple_of(j * TK, TK)
        for h in range(heads):
            sl = slice(h * LANES, (h + 1) * LANES)
            s = _dot(k_ref[0, pl.ds(start, TK), sl], q_ref[0, sl, :])
            _store_scores(s, s_ref, smax_ref, slot, h, masked)

    def consume(j, slot):
        for h in range(heads):
            _softmax_tile_update(vt_ref[0, h, j], s_ref, smax_ref, m_ref, acc_ref, slot, h)

    _pipelined_key_sweep(i, scores, consume)
    for h in range(heads):
        o_ref[0, h * MLA_V:(h + 1) * MLA_V, :] = _softmax_result(acc_ref, h, MLA_V)


def _mla_attention(qm, km, vmt):
    B, S, _ = km.shape
    nt = S // TK
    hb = MLA_HEADS_PER_STEP
    return pl.pallas_call(
        _mla_kernel,
        grid=(B, MLA_HEADS // hb, S // TQ),
        in_specs=[
            pl.BlockSpec((1, hb * LANES, TQ), lambda b, h, i: (b, h, i)),
            pl.BlockSpec((1, S, hb * LANES), lambda b, h, i: (b, 0, h)),
            pl.BlockSpec((1, hb, nt, VT_ROWS, TK), lambda b, h, i: (b, h, 0, 0, 0)),
        ],
        out_specs=pl.BlockSpec((1, hb * MLA_V, TQ), lambda b, h, i: (b, h, i)),
        out_shape=jax.ShapeDtypeStruct((B, MLA_WIDTH, S), F32),
        scratch_shapes=_softmax_scratch(hb, TQ),
        compiler_params=pltpu.CompilerParams(
            dimension_semantics=("parallel", "parallel", "arbitrary"),
            vmem_limit_bytes=VMEM_LIMIT),
        name="mla_attn",
    )(qm, km, vmt)


def _diff_kernel(ordered_ref, q_ref, k_ref, vt_ref, posq_ref, posk_ref, slope_ref, lam_ref, g_ref,
                 o_ref, qq_ref, s_ref, smax_ref, m_ref, acc_ref, *, lam_init):
    b = pl.program_id(0)
    i = pl.program_id(2)
    heads = m_ref.shape[0]
    _softmax_state_init(m_ref, acc_ref)
    dim = lax.broadcasted_iota(jnp.int32, (LANES, TQ), 0)
    pos_q = posq_ref[0]

    def stack_queries(with_pos_cols):
        for h in range(heads):
            q = q_ref[0, h * LANES:(h + 1) * LANES, :]
            for mp in range(2):
                keep = (dim >= mp * DIFF_QK) & (dim < (mp + 1) * DIFF_QK)
                if with_pos_cols:
                    keep = keep | (dim >= POS_COL0)
                qq_ref[h, :, mp * TQ:(mp + 1) * TQ] = jnp.where(keep, q, jnp.zeros_like(q))

    def make_scores(ordered):
        def scores(j, slot, masked):
            start = pl.multiple_of(j * TK, TK)
            vector_bias = masked or not ordered
            if vector_bias:
                d = posk_ref[0, pl.ds(start, TK), :] - pos_q
                dist = 2.0 * jnp.maximum(d, 0.0) if ordered else jnp.abs(d)
            for h in range(heads):
                s = _dot(k_ref[0, pl.ds(start, TK), h * LANES:(h + 1) * LANES], qq_ref[h])
                if vector_bias:
                    bias = dist * slope_ref[h]
                    s = s - jnp.concatenate([bias, bias], axis=1)
                _store_scores(s, s_ref, smax_ref, slot, h, masked)
        return scores

    def consume(j, slot):
        for h in range(heads):
            _softmax_tile_update(vt_ref[0, h, j], s_ref, smax_ref, m_ref, acc_ref, slot, h)

    @pl.when(ordered_ref[b, i] != 0)
    def _():
        stack_queries(True)
        _pipelined_key_sweep(i, make_scores(True), consume)

    @pl.when(ordered_ref[b, i] == 0)
    def _():
        stack_queries(False)
        _pipelined_key_sweep(i, make_scores(False), consume)

    lam = lam_ref[...]
    lam_full = (jnp.exp(jnp.sum(lam[0:1] * lam[1:2], keepdims=True))
                - jnp.exp(jnp.sum(lam[2:3] * lam[3:4], keepdims=True)) + lam_init)
    for h in range(heads):
        o = _softmax_result(acc_ref, h, DIFF_V)
        od = o[:, :TQ] - lam_full * o[:, TQ:]
        o_ref[0, h * DIFF_V:(h + 1) * DIFF_V, :] = _rms(od, g_ref[h], axis=0) * (1.0 - lam_init)


def _diff_attention(ordered, qd, kd, vdt, rel_row, rel_col, slopes, lam, sub_norm, lam_init):
    B, S, _ = kd.shape
    nt = S // TK
    hb = DIFF_HEADS_PER_STEP
    once = pl.Buffered(1)
    return pl.pallas_call(
        functools.partial(_diff_kernel, lam_init=lam_init),
        grid=(B, DIFF_HEADS // hb, S // TQ),
        in_specs=[
            pl.BlockSpec(memory_space=pltpu.SMEM),
            pl.BlockSpec((1, hb * LANES, TQ), lambda b, h, i: (b, h, i)),
            pl.BlockSpec((1, S, hb * LANES), lambda b, h, i: (b, 0, h), pipeline_mode=once),
            pl.BlockSpec((1, hb, nt, VT_ROWS, TK), lambda b, h, i: (b, h, 0, 0, 0),
                         pipeline_mode=once),
            pl.BlockSpec((1, 1, TQ), lambda b, h, i: (b, 0, i)),
            pl.BlockSpec((1, S, 1), lambda b, h, i: (b, 0, 0), pipeline_mode=once),
            pl.BlockSpec((hb, 1, 1), lambda b, h, i: (h, 0, 0)),
            pl.BlockSpec(lam.shape, lambda b, h, i: (0, 0)),
            pl.BlockSpec((hb, DIFF_V, 1), lambda b, h, i: (h, 0, 0)),
        ],
        out_specs=pl.BlockSpec((1, hb * DIFF_V, TQ), lambda b, h, i: (b, h, i)),
        out_shape=jax.ShapeDtypeStruct((B, DIFF_WIDTH, S), F32),
        scratch_shapes=[pltpu.VMEM((hb, LANES, 2 * TQ), BF16)] + _softmax_scratch(hb, 2 * TQ),
        compiler_params=pltpu.CompilerParams(
            dimension_semantics=("parallel", "parallel", "arbitrary"),
            vmem_limit_bytes=VMEM_LIMIT),
        name="diff_attn",
    )(ordered, qd, kd, vdt, rel_row, rel_col, slopes, lam, sub_norm)


CH_PAD = CH_LEFT * CHUNK
CH_ROWS = 256


def _chunk_kernel(q_ref, k_ref, v_ref, diag_ref, o_ref, bias_ref):
    i = pl.program_id(1)
    heads_per_group = LANES // CH_HD
    groups = CH_WIDTH // LANES
    chunks = CH_ROWS // CHUNK

    @pl.when(i == 0)
    def _():
        for h in range(CH_HEADS):
            rows = jnp.broadcast_to(diag_ref[h:h + 1, :], (CHUNK, diag_ref.shape[1]))
            bias_ref[h] = pltpu.roll(rows, 0, 1, stride=1, stride_axis=0)[:, :CH_BAND]

    lane = lax.broadcasted_iota(jnp.int32, (CHUNK, LANES), 1)
    key_col = lax.broadcasted_iota(jnp.int32, (CHUNK, CH_BAND), 1)
    blocks = [(c, g) for c in range(chunks) for g in range(groups)]
    starts = [pl.multiple_of((i * chunks + c) * CHUNK, CHUNK) for c in range(chunks)]

    scores = []
    for c, g in blocks:
        gl = slice(g * LANES, (g + 1) * LANES)
        q = q_ref[0, c * CHUNK:(c + 1) * CHUNK, gl]
        qq = jnp.concatenate(
            [jnp.where((lane >= hh * CH_HD) & (lane < (hh + 1) * CH_HD), q, jnp.zeros_like(q))
             for hh in range(heads_per_group)], axis=0)
        scores.append(_dot_nt(qq, k_ref[0, pl.ds(starts[c], CH_BAND), gl]))

    probs, denoms = [], []
    for (c, g), s in zip(blocks, scores):
        valid = key_col >= CH_PAD - starts[c]
        s = jnp.concatenate(
            [jnp.where(valid, s[hh * CHUNK:(hh + 1) * CHUNK] + bias_ref[g * heads_per_group + hh],
                       -jnp.inf) for hh in range(heads_per_group)], axis=0)
        p = jnp.exp2(s - jnp.max(s, axis=1, keepdims=True))
        denoms.append(jnp.sum(p, axis=1, keepdims=True))
        probs.append(p.astype(BF16))

    for (c, g), p, l in zip(blocks, probs, denoms):
        gl = slice(g * LANES, (g + 1) * LANES)
        o = _dot(p, v_ref[0, pl.ds(starts[c], CH_BAND), gl]) / l
        o_ref[0, c * CHUNK:(c + 1) * CHUNK, gl] = jnp.where(lane < CH_HD, o[:CHUNK], o[CHUNK:])


def _chunk_attention(qc, kc_pad, vc_pad, diag):
    B, S, _ = qc.shape
    return pl.pallas_call(
        _chunk_kernel,
        grid=(B, S // CH_ROWS),
        in_specs=[
            pl.BlockSpec((1, CH_ROWS, CH_WIDTH), lambda b, i: (b, i, 0)),
            pl.BlockSpec((1, S + CH_PAD, CH_WIDTH), lambda b, i: (b, 0, 0)),
            pl.BlockSpec((1, S + CH_PAD, CH_WIDTH), lambda b, i: (b, 0, 0)),
            pl.BlockSpec(diag.shape, lambda b, i: (0, 0)),
        ],
        out_specs=pl.BlockSpec((1, CH_ROWS, CH_WIDTH), lambda b, i: (b, i, 0)),
        out_shape=jax.ShapeDtypeStruct((B, S, CH_WIDTH), F32),
        scratch_shapes=[pltpu.VMEM((CH_HEADS, CHUNK, CH_BAND), F32)],
        compiler_params=pltpu.CompilerParams(
            dimension_semantics=("parallel", "arbitrary"), vmem_limit_bytes=VMEM_LIMIT),
        name="chunk_attn",
    )(qc, kc_pad, vc_pad, diag)


def _outproj_kernel(omt_ref, odt_ref, oc_ref, x_ref, gm_ref, gc_ref, w_ref, o_ref):
    om = _rms(omt_ref[0], gm_ref[...], axis=0).T.astype(BF16)
    od = odt_ref[0].T.astype(BF16)
    oc = _rms(oc_ref[0], gc_ref[...]).astype(BF16)
    mix = jnp.concatenate([om, od, oc], axis=1)
    o_ref[0] = x_ref[0] + _dot(mix, w_ref[...])


def _outproj(omt, odt, oc, x, gm, gc, w):
    B, S, D = x.shape
    tm = min(ROW_TILE, S)
    full = lambda a: pl.BlockSpec(a.shape, lambda b, i: (0,) * a.ndim)
    return pl.pallas_call(
        _outproj_kernel,
        grid=(B, S // tm),
        in_specs=[
            pl.BlockSpec((1, MLA_WIDTH, tm), lambda b, i: (b, 0, i)),
            pl.BlockSpec((1, DIFF_WIDTH, tm), lambda b, i: (b, 0, i)),
            pl.BlockSpec((1, tm, CH_WIDTH), lambda b, i: (b, i, 0)),
            pl.BlockSpec((1, tm, D), lambda b, i: (b, i, 0)),
            full(gm), full(gc), full(w),
        ],
        out_specs=pl.BlockSpec((1, tm, D), lambda b, i: (b, i, 0)),
        out_shape=jax.ShapeDtypeStruct((B, S, D), F32),
        compiler_params=pltpu.CompilerParams(
            dimension_semantics=("parallel", "parallel"), vmem_limit_bytes=VMEM_LIMIT),
        name="outproj",
    )(omt, odt, oc, x, gm, gc, w)


HALO = 8


def _ffn_kernel(x_ref, g_ref, wi_ref, cw_ref, cb_ref, wo_ref, gf_ref, o_ref, h_ref, carry_ref,
                *, final_norm):
    i = pl.program_id(1)
    tm = x_ref.shape[1]

    @pl.when(i == 0)
    def _():
        carry_ref[...] = jnp.zeros(carry_ref.shape, F32)

    x = x_ref[0]
    xn = _rms(x, g_ref[...]).astype(BF16)

    def project(c):
        lo, w = FF_CHUNKS[c]
        for half in range(2):
            col = half * D_FF + lo
            h_ref[c % 2, half, HALO:HALO + tm, 0:w] = _dot(xn, wi_ref[:, col:col + w])

    def gate(c):
        lo, w = FF_CHUNKS[c]
        halves = []
        for half in range(2):
            col = half * D_FF + lo
            h = h_ref.at[c % 2, half]
            h[0:HALO, 0:w] = carry_ref[half, :, lo:lo + w]
            carry_ref[half, :, lo:lo + w] = h[tm:tm + HALO, 0:w]
            cw = cw_ref[:, col:col + w]
            halves.append(cw[0:1] * h[HALO - 2:HALO - 2 + tm, 0:w]
                          + cw[1:2] * h[HALO - 1:HALO - 1 + tm, 0:w]
                          + cw[2:3] * h[HALO:HALO + tm, 0:w] + cb_ref[:, col:col + w])
        u, g = halves
        return (g / (1.0 + jnp.exp(-g)) * u).astype(BF16)

    project(0)
    y = x
    for c, (lo, w) in enumerate(FF_CHUNKS):
        if c + 1 < len(FF_CHUNKS):
            project(c + 1)
        y = y + _dot(gate(c), wo_ref[lo:lo + w, :])
    if final_norm:
        y = _rms(y, gf_ref[...])
    o_ref[0] = y


def _ffn(x, g, w_in, conv_w, conv_b, w_out, g_final, final_norm):
    B, S, D = x.shape
    tm = min(ROW_TILE, S)
    resident = lambda a: pl.BlockSpec(a.shape, lambda b, i: (0,) * a.ndim,
                                      pipeline_mode=pl.Buffered(1))
    chunk_w = max(w for _, w in FF_CHUNKS)
    return pl.pallas_call(
        functools.partial(_ffn_kernel, final_norm=final_norm),
        grid=(B, S // tm),
        in_specs=[
            pl.BlockSpec((1, tm, D), lambda b, i: (b, i, 0)),
            resident(g), resident(w_in), resident(conv_w), resident(conv_b), resident(w_out),
            resident(g_final),
        ],
        out_specs=pl.BlockSpec((1, tm, D), lambda b, i: (b, i, 0)),
        out_shape=jax.ShapeDtypeStruct((B, S, D), F32),
        scratch_shapes=[
            pltpu.VMEM((2, 2, HALO + tm, chunk_w), F32),
            pltpu.VMEM((2, HALO, D_FF), F32),
        ],
        compiler_params=pltpu.CompilerParams(
            dimension_semantics=("parallel", "arbitrary"), vmem_limit_bytes=VMEM_LIMIT),
        name="ffn",
    )(x, g, w_in, conv_w, conv_b, w_out, g_final)


def _rotate_half_cols(w):
    half = w.shape[1] // 2
    return jnp.concatenate([-w[:, half:], w[:, :half]], axis=1)


def _place(cols, offset, width=LANES):
    return jnp.pad(cols, ((0, 0), (offset, width - offset - cols.shape[1])))


def _layout_w_in(w):
    kr = w[:, MLA_Q_RANK + MLA_KV_RANK:IN_MLA]
    return jnp.concatenate([
        w[:, :MLA_Q_RANK + MLA_KV_RANK],
        _place(kr, MLA_NOPE), _place(_rotate_half_cols(kr), MLA_NOPE),
        w[:, IN_MLA:],
    ], axis=1).astype(BF16)


def _layout_w_uq(w):
    hd = MLA_NOPE + MLA_ROPE
    plain, rot = [], []
    for h in range(MLA_HEADS):
        wh = w[:, h * hd:(h + 1) * hd]
        plain.append(_place(wh, 0))
        rot.append(_place(_rotate_half_cols(wh[:, MLA_NOPE:]), MLA_NOPE))
    return jnp.concatenate(plain + rot, axis=1).astype(BF16)


def _layout_w_ukv(w):
    hd = MLA_NOPE + MLA_V
    ks = [_place(w[:, h * hd:h * hd + MLA_NOPE], 0) for h in range(MLA_HEADS)]
    vs = [w[:, h * hd + MLA_NOPE:(h + 1) * hd] for h in range(MLA_HEADS)]
    return jnp.concatenate(ks + vs, axis=1).astype(BF16)


def _slope_columns(slopes2):
    p1 = slopes2.astype(BF16).astype(F32)
    p2 = (slopes2 - p1).astype(BF16).astype(F32)
    p3 = (slopes2 - p1 - p2).astype(BF16).astype(F32)
    pieces = jnp.stack([p1, p2, p3], axis=1)
    cols = jnp.concatenate([pieces * 65536.0, pieces * 256.0, pieces], axis=1)
    return _place(cols, POS_COL0).reshape(1, -1)


def _ordered_query_tiles(rel):
    B, S = rel.shape
    tiles = rel.reshape(B, S // TQ, TQ)
    tile_max, tile_min = jnp.max(tiles, axis=2), jnp.min(tiles, axis=2)
    before = lax.cummax(tile_max, axis=1)
    before = jnp.concatenate([tile_min[:, :1], before[:, :-1]], axis=1)
    return (before <= tile_min).astype(jnp.int32)


def _chunk_bias_diagonals(rel_bias):
    n_rel = rel_bias.shape[1]
    far = jnp.broadcast_to(rel_bias[:, n_rel - 1:], (rel_bias.shape[0], REL_MAX))
    wrap = far[:, :CHUNK]
    return jnp.concatenate([far, rel_bias[:, ::-1], wrap], axis=1) * LOG2E


def kernel(x, positions, attn_norm, w_in, mla_q_norm, mla_w_uq, mla_kv_norm, mla_w_ukv, diff_lambda, diff_norm, chunk_rel_bias, mla_out_norm, chunk_out_norm, w_out, ffn_norm, w_ffn_in, ffn_conv_w, ffn_conv_b, w_ffn_out, final_norm):
    B, S, D = x.shape
    pos_col = positions.astype(F32)[:, :, None]
    rel = positions - positions[:, :1]
    rel_f = rel.astype(F32)
    half = MLA_ROPE // 2
    inv = ROPE_THETA ** (-jnp.arange(half, dtype=F32) / half)
    inv_slot = _place(jnp.concatenate([inv, inv])[None, :], MLA_NOPE)
    slopes = 2.0 ** (-8.0 * jnp.arange(1, DIFF_HEADS + 1, dtype=F32) / DIFF_HEADS)
    slopes2 = slopes * LOG2E
    ordered = _ordered_query_tiles(rel)
    slope_cols = _slope_columns(slopes2)
    slopes2 = slopes2.reshape(DIFF_HEADS, 1, 1)
    row = lambda v: v.reshape(1, -1)

    for l in range(DEPTH):
        lam_init = 0.8 - 0.6 * math.exp(-0.3 * l)
        qm, km, vmt, qd, kd, vdt, qc, kc, vc = _inproj(
            x, pos_col, rel[:, :, None], row(attn_norm[l]), _layout_w_in(w_in[l]),
            row(mla_q_norm[l]), _layout_w_uq(mla_w_uq[l]), row(mla_kv_norm[l]),
            _layout_w_ukv(mla_w_ukv[l]), inv_slot, slope_cols)
        omt = _mla_attention(qm, km, vmt)
        odt = _diff_attention(ordered, qd, kd, vdt, rel_f[:, None, :], rel_f[:, :, None], slopes2,
                              diff_lambda[l], diff_norm[l].reshape(DIFF_HEADS, DIFF_V, 1), lam_init)
        pad = ((0, 0), (CH_PAD, 0), (0, 0))
        oc = _chunk_attention(qc, jnp.pad(kc, pad), jnp.pad(vc, pad),
                              _chunk_bias_diagonals(chunk_rel_bias[l]))
        x = _outproj(omt, odt, oc, x, mla_out_norm[l].reshape(-1, 1), row(chunk_out_norm[l]),
                     w_out[l].astype(BF16))
        x = _ffn(x, row(ffn_norm[l]), w_ffn_in[l].astype(BF16), ffn_conv_w[l],
                 row(ffn_conv_b[l]), w_ffn_out[l].astype(BF16), row(final_norm),
                 final_norm=(l == DEPTH - 1))
    return x
```

```python
import functools
import math

import jax
import jax.numpy as jnp
from jax import lax
from jax.experimental import pallas as pl
from jax.experimental.pallas import tpu as pltpu

F32 = jnp.float32
BF16 = jnp.bfloat16

D_MODEL = 1024
DEPTH = 2
CHUNK = 64
EPS = 1e-6

MLA_HEADS = 6
MLA_Q_RANK = 256
MLA_KV_RANK = 128
MLA_NOPE = 64
MLA_ROPE = 32
MLA_V = 64
ROPE_THETA = 10000.0

DIFF_HEADS = 6
DIFF_QK = 32
DIFF_V = 64

CH_HEADS = 4
CH_HD = 64
CH_LEFT = 8
CH_BAND = (CH_LEFT + 1) * CHUNK
REL_MAX = 256

MLA_WIDTH = MLA_HEADS * MLA_V
DIFF_WIDTH = DIFF_HEADS * DIFF_V
CH_WIDTH = CH_HEADS * CH_HD
DIFF_QK_WIDTH = DIFF_HEADS * 2 * DIFF_QK

IN_MLA = MLA_Q_RANK + MLA_KV_RANK + MLA_ROPE
D_FF = 2816

LOG2E = math.log2(math.e)

LANES = 128

ROW_TILE = 512
TQ = 512
TK = 256
FF_CHUNKS = ((0, 1024), (1024, 1024), (2048, 768))
VT_ROWS = 80
MLA_HEADS_PER_STEP = 6
DIFF_HEADS_PER_STEP = 6
POS_COL0 = 2 * DIFF_QK
VMEM_LIMIT = 56 * 1024 * 1024

P_CQ = 0
P_CKV = MLA_Q_RANK
P_KR = P_CKV + MLA_KV_RANK
P_KRR = P_KR + LANES
P_DQ = P_KRR + LANES
P_DK = P_DQ + DIFF_QK_WIDTH
P_DV = P_DK + DIFF_QK_WIDTH
P_CQ_ = P_DV + DIFF_WIDTH
P_CK_ = P_CQ_ + CH_WIDTH
P_CV_ = P_CK_ + CH_WIDTH
P_TOTAL = P_CV_ + CH_WIDTH


def _rms(x, g, axis=-1):
    return x * lax.rsqrt(jnp.mean(x * x, axis=axis, keepdims=True) + EPS) * g


def _dot(a, b):
    return jnp.dot(a, b, preferred_element_type=F32)


def _dot_nt(a, b):
    return lax.dot_general(a, b, (((1,), (1,)), ((), ())), preferred_element_type=F32)


def _inproj_kernel(x_ref, pos_ref, rel_ref, g_ref, win_ref, qn_ref, wq_ref, kvn_ref, wkv_ref,
                   inv_ref, qcols_ref, qm_ref, km_ref, vmt_ref, qd_ref, kd_ref, vdt_ref, qc_ref, kc_ref, vc_ref):
    tm = x_ref.shape[1]
    xn = _rms(x_ref[0], g_ref[...]).astype(BF16)
    proj = _dot(xn, win_ref[...])

    ang = pos_ref[0] * inv_ref[...]
    cos = jnp.cos(ang)
    sin = jnp.sin(ang)

    cqn = _rms(proj[:, P_CQ:P_CKV], qn_ref[...]).astype(BF16)
    q2 = _dot(cqn, wq_ref[...])
    ckvn = _rms(proj[:, P_CKV:P_KR], kvn_ref[...]).astype(BF16)
    kv2 = _dot(ckvn, wkv_ref[...])

    k_rope = proj[:, P_KR:P_KRR] * cos + proj[:, P_KRR:P_DQ] * sin
    q_scale = (MLA_NOPE + MLA_ROPE) ** -0.5 * LOG2E
    nh = MLA_HEADS * LANES
    q_heads = []
    for h in range(MLA_HEADS):
        sl = slice(h * LANES, (h + 1) * LANES)
        qh = q2[:, sl] * cos + q2[:, nh + h * LANES: nh + (h + 1) * LANES] * sin
        q_heads.append(qh * q_scale)
        km_ref[0, :, sl] = (kv2[:, sl] + k_rope).astype(BF16)
    qm_ref[0] = jnp.concatenate(q_heads, axis=1).T.astype(BF16)

    vm_t = kv2[:, nh:nh + MLA_WIDTH].T.astype(BF16)
    vd_t = proj[:, P_DV:P_CQ_].T.astype(BF16)
    ones_rows = jnp.where(lax.broadcasted_iota(jnp.int32, (VT_ROWS - MLA_V, TK), 0) == 0,
                          1.0, 0.0).astype(BF16)
    for h in range(MLA_HEADS):
        for t in range(tm // TK):
            vmt_ref[0, h, t, 0:MLA_V, :] = vm_t[h * MLA_V:(h + 1) * MLA_V, t * TK:(t + 1) * TK]
            vdt_ref[0, h, t, 0:DIFF_V, :] = vd_t[h * DIFF_V:(h + 1) * DIFF_V, t * TK:(t + 1) * TK]
            vmt_ref[0, h, t, MLA_V:VT_ROWS, :] = ones_rows
            vdt_ref[0, h, t, DIFF_V:VT_ROWS, :] = ones_rows

    lane = lax.broadcasted_iota(jnp.int32, (1, LANES), 1)
    rel = rel_ref[0]
    digits = [(rel >> 16).astype(F32), ((rel >> 8) & 255).astype(F32), (rel & 255).astype(F32)]
    in_cols = (lane >= POS_COL0) & (lane < POS_COL0 + 9)
    k_cols = jnp.where(in_cols, jnp.where(lane < POS_COL0 + 3, digits[0],
                                          jnp.where(lane < POS_COL0 + 6, digits[1], digits[2])), 0.0)
    qd = proj[:, P_DQ:P_DK] * (DIFF_QK ** -0.5 * LOG2E)
    kd = proj[:, P_DK:P_DV]
    heads_per_group = LANES // (2 * DIFF_QK)
    q_heads = []
    for h in range(DIFF_HEADS):
        gl = slice((h // heads_per_group) * LANES, (h // heads_per_group + 1) * LANES)
        sl = slice(h * LANES, (h + 1) * LANES)
        qh, kh = qd[:, gl], kd[:, gl]
        if h % heads_per_group:
            qh, kh = pltpu.roll(qh, 2 * DIFF_QK, 1), pltpu.roll(kh, 2 * DIFF_QK, 1)
        q_heads.append(jnp.where(lane < 2 * DIFF_QK, qh, qcols_ref[:, sl]))
        kd_ref[0, :, sl] = jnp.where(lane < 2 * DIFF_QK, kh, k_cols).astype(BF16)
    qd_ref[0] = jnp.concatenate(q_heads, axis=1).T.astype(BF16)
    qc_ref[0] = (proj[:, P_CQ_:P_CK_] * (CH_HD ** -0.5 * LOG2E)).astype(BF16)
    kc_ref[0] = proj[:, P_CK_:P_CV_].astype(BF16)
    vc_ref[0] = proj[:, P_CV_:P_TOTAL].astype(BF16)


def _inproj(x, pos_col, rel_col, g, win, qn, wq, kvn, wkv, inv, qcols):
    B, S, D = x.shape
    tm = min(ROW_TILE, S)
    nt = S // TK
    full = lambda a: pl.BlockSpec(a.shape, lambda b, i: (0,) * a.ndim)
    row = lambda w: pl.BlockSpec((1, tm, w), lambda b, i: (b, i, 0))
    col = lambda w: pl.BlockSpec((1, w, tm), lambda b, i: (b, 0, i))
    vt_spec =pl.BlockSpec((1, MLA_HEADS, tm // TK, VT_ROWS, TK), lambda b, i: (b, 0, i, 0, 0))
    out_shape = (
        jax.ShapeDtypeStruct((B, MLA_HEADS * LANES, S), BF16),
        jax.ShapeDtypeStruct((B, S, MLA_HEADS * LANES), BF16),
        jax.ShapeDtypeStruct((B, MLA_HEADS, nt, VT_ROWS, TK), BF16),
        jax.ShapeDtypeStruct((B, DIFF_HEADS * LANES, S), BF16),
        jax.ShapeDtypeStruct((B, S, DIFF_HEADS * LANES), BF16),
        jax.ShapeDtypeStruct((B, DIFF_HEADS, nt, VT_ROWS, TK), BF16),
        jax.ShapeDtypeStruct((B, S, CH_WIDTH), BF16),
        jax.ShapeDtypeStruct((B, S, CH_WIDTH), BF16),
        jax.ShapeDtypeStruct((B, S, CH_WIDTH), BF16),
    )
    return pl.pallas_call(
        _inproj_kernel,
        grid=(B, S // tm),
        in_specs=[row(D), row(1), row(1), full(g), full(win), full(qn), full(wq), full(kvn),
                  full(wkv), full(inv), full(qcols)],
        out_specs=(col(MLA_HEADS * LANES), row(MLA_HEADS * LANES), vt_spec,
                   col(DIFF_HEADS * LANES), row(DIFF_HEADS * LANES), vt_spec,
                   row(CH_WIDTH), row(CH_WIDTH), row(CH_WIDTH)),
        out_shape=out_shape,
        compiler_params=pltpu.CompilerParams(
            dimension_semantics=("parallel", "parallel"), vmem_limit_bytes=VMEM_LIMIT),
        name="inproj",
    )(x, pos_col, rel_col, g, win, qn, wq, kvn, wkv, inv, qcols)


def _chunk_causal_tile_mask(n_q, key_offset):
    r = (lax.broadcasted_iota(jnp.int32, (TK, n_q), 0) + key_offset) // CHUNK
    c = (lax.broadcasted_iota(jnp.int32, (TK, n_q), 1) % TQ) // CHUNK
    return r <= c


def _store_scores(s, s_ref, smax_ref, slot, h, diag):
    if diag is not None:
        s = jnp.where(_chunk_causal_tile_mask(s.shape[1], diag), s, -jnp.inf)
    s_ref[slot, h] = s
    smax_ref[slot, h] = jnp.max(s, axis=0, keepdims=True)


def _softmax_tile_update(vt, s_ref, smax_ref, m_ref, acc_ref, slot, h):
    m = m_ref[h]
    m_new = jnp.maximum(m, smax_ref[slot, h])
    p = jnp.exp2(s_ref[slot, h] - m_new).astype(BF16)
    acc_ref[h] = jnp.exp2(m - m_new) * acc_ref[h] + _dot(vt, p)
    m_ref[h] = m_new


def _softmax_state_init(m_ref, acc_ref):
    m_ref[...] = jnp.full(m_ref.shape, -jnp.inf, F32)
    acc_ref[...] = jnp.zeros(acc_ref.shape, F32)


def _softmax_result(acc_ref, h, dv):
    acc = acc_ref[h]
    return acc[0:dv] / acc[dv:dv + 1]


def _pipelined_key_sweep(i, scores, consume):
    assert TQ == 2 * TK

    @pl.when(i == 0)
    def _():
        scores(0, 0, 0)
        scores(1, 1, TK)
        consume(0, 0)
        consume(1, 1)

    @pl.when(i > 0)
    def _():
        scores(0, 0, None)

        def body(jj, carry):
            j = 2 * jj
            scores(j + 1, 1, None)
            consume(j, 0)
            scores(j + 2, 0, None)
            consume(j + 1, 1)
            return carry

        lax.fori_loop(0, i - 1, body, 0)
        j = 2 * i - 2
        scores(j + 1, 1, None)
        consume(j, 0)
        scores(j + 2, 0, 0)
        consume(j + 1, 1)
        scores(j + 3, 1, TK)
        consume(j + 2, 0)
        consume(j + 3, 1)


def _softmax_scratch(heads, n):
    return [pltpu.VMEM((2, heads, TK, n), F32), pltpu.VMEM((2, heads, 1, n), F32),
            pltpu.VMEM((heads, 1, n), F32), pltpu.VMEM((heads, VT_ROWS, n), F32)]


def _mla_kernel(q_ref, k_ref, vt_ref, o_ref, s_ref, smax_ref, m_ref, acc_ref):
    i = pl.program_id(2)
    heads = m_ref.shape[0]
    _softmax_state_init(m_ref, acc_ref)

    def scores(j, slot, diag):
        start = pl.multiple_of(j * TK, TK)
        for h in range(heads):
            sl = slice(h * LANES, (h + 1) * LANES)
            s = _dot(k_ref[0, pl.ds(start, TK), sl], q_ref[0, sl, :])
            _store_scores(s, s_ref, smax_ref, slot, h, diag)

    def consume(j, slot):
        for h in range(heads):
            _softmax_tile_update(vt_ref[0, h, j], s_ref, smax_ref, m_ref, acc_ref, slot, h)

    _pipelined_key_sweep(i, scores, consume)
    for h in range(heads):
        o_ref[0, h * MLA_V:(h + 1) * MLA_V, :] = _softmax_result(acc_ref, h, MLA_V)


def _mla_attention(qm, km, vmt):
    B, S, _ = km.shape
    nt = S // TK
    hb = MLA_HEADS_PER_STEP
    return pl.pallas_call(
        _mla_kernel,
        grid=(B, MLA_HEADS // hb, S // TQ),
        in_specs=[
            pl.BlockSpec((1, hb * LANES, TQ), lambda b, h, i: (b, h, i)),
            pl.BlockSpec((1, S, hb * LANES), lambda b, h, i: (b, 0, h)),
            pl.BlockSpec((1, hb, nt, VT_ROWS, TK), lambda b, h, i: (b, h, 0, 0, 0)),
        ],
        out_specs=pl.BlockSpec((1, hb * MLA_V, TQ), lambda b, h, i: (b, h, i)),
        out_shape=jax.ShapeDtypeStruct((B, MLA_WIDTH, S), F32),
        scratch_shapes=_softmax_scratch(hb, TQ),
        compiler_params=pltpu.CompilerParams(
            dimension_semantics=("parallel", "parallel", "arbitrary"),
            vmem_limit_bytes=VMEM_LIMIT),
        name="mla_attn",
    )(qm, km, vmt)


def _diff_kernel(ordered_ref, q_ref, k_ref, vt_ref, posq_ref, posk_ref, slope_ref, lam_ref, g_ref,
                 o_ref, qq_ref, s_ref, smax_ref, m_ref, acc_ref, *, lam_init):
    b = pl.program_id(0)
    i = pl.program_id(2)
    heads = m_ref.shape[0]
    _softmax_state_init(m_ref, acc_ref)
    dim = lax.broadcasted_iota(jnp.int32, (LANES, TQ), 0)
    pos_q = posq_ref[0]

    def stack_queries(with_pos_cols):
        for h in range(heads):
            q = q_ref[0, h * LANES:(h + 1) * LANES, :]
            for mp in range(2):
                keep = (dim >= mp * DIFF_QK) & (dim < (mp + 1) * DIFF_QK)
                if with_pos_cols:
                    keep = keep | (dim >= POS_COL0)
                qq_ref[h, :, mp * TQ:(mp + 1) * TQ] = jnp.where(keep, q, jnp.zeros_like(q))

    def make_scores(ordered):
        def scores(j, slot, diag):
            start = pl.multiple_of(j * TK, TK)
            vector_bias = diag is not None or not ordered
            if vector_bias:
                d = posk_ref[0, pl.ds(start, TK), :] - pos_q
                dist = 2.0 * jnp.maximum(d, 0.0) if ordered else jnp.abs(d)
            for h in range(heads):
                s = _dot(k_ref[0, pl.ds(start, TK), h * LANES:(h + 1) * LANES], qq_ref[h])
                if vector_bias:
                    bias = dist * slope_ref[h]
                    s = s - jnp.concatenate([bias, bias], axis=1)
                _store_scores(s, s_ref, smax_ref, slot, h, diag)
        return scores

    def consume(j, slot):
        for h in range(heads):
            _softmax_tile_update(vt_ref[0, h, j], s_ref, smax_ref, m_ref, acc_ref, slot, h)

    @pl.when(ordered_ref[b, i] != 0)
    def _():
        stack_queries(True)
        _pipelined_key_sweep(i, make_scores(True), consume)

    @pl.when(ordered_ref[b, i] == 0)
    def _():
        stack_queries(False)
        _pipelined_key_sweep(i, make_scores(False), consume)

    lam = lam_ref[...]
    lam_full = (jnp.exp(jnp.sum(lam[0:1] * lam[1:2], keepdims=True))
                - jnp.exp(jnp.sum(lam[2:3] * lam[3:4], keepdims=True)) + lam_init)
    for h in range(heads):
        o = _softmax_result(acc_ref, h, DIFF_V)
        od = o[:, :TQ] - lam_full * o[:, TQ:]
        o_ref[0, h * DIFF_V:(h + 1) * DIFF_V, :] = _rms(od, g_ref[h], axis=0) * (1.0 - lam_init)


def _diff_attention(ordered, qd, kd, vdt, rel_row, rel_col, slopes, lam, sub_norm, lam_init):
    B, S, _ = kd.shape
    nt = S // TK
    hb = DIFF_HEADS_PER_STEP
    once = pl.Buffered(1)
    return pl.pallas_call(
        functools.partial(_diff_kernel, lam_init=lam_init),
        grid=(B, DIFF_HEADS // hb, S // TQ),
        in_specs=[
            pl.BlockSpec(memory_space=pltpu.SMEM),
            pl.BlockSpec((1, hb * LANES, TQ), lambda b, h, i: (b, h, i)),
            pl.BlockSpec((1, S, hb * LANES), lambda b, h, i: (b, 0, h), pipeline_mode=once),
            pl.BlockSpec((1, hb, nt, VT_ROWS, TK), lambda b, h, i: (b, h, 0, 0, 0),
                         pipeline_mode=once),
            pl.BlockSpec((1, 1, TQ), lambda b, h, i: (b, 0, i)),
            pl.BlockSpec((1, S, 1), lambda b, h, i: (b, 0, 0), pipeline_mode=once),
            pl.BlockSpec((hb, 1, 1), lambda b, h, i: (h, 0, 0)),
            pl.BlockSpec(lam.shape, lambda b, h, i: (0, 0)),
            pl.BlockSpec((hb, DIFF_V, 1), lambda b, h, i: (h, 0, 0)),
        ],
        out_specs=pl.BlockSpec((1, hb * DIFF_V, TQ), lambda b, h, i: (b, h, i)),
        out_shape=jax.ShapeDtypeStruct((B, DIFF_WIDTH, S), F32),
        scratch_shapes=[pltpu.VMEM((hb, LANES, 2 * TQ), BF16)] + _softmax_scratch(hb, 2 * TQ),
        compiler_params=pltpu.CompilerParams(
            dimension_semantics=("parallel", "parallel", "arbitrary"),
            vmem_limit_bytes=VMEM_LIMIT),
        name="diff_attn",
    )(ordered, qd, kd, vdt, rel_row, rel_col, slopes, lam, sub_norm)


CH_PAD = CH_LEFT * CHUNK
CH_ROWS = 256


def _chunk_kernel(q_ref, k_ref, v_ref, diag_ref, o_ref, bias_ref):
    i = pl.program_id(1)
    heads_per_group = LANES // CH_HD
    groups = CH_WIDTH // LANES
    chunks = CH_ROWS // CHUNK

    @pl.when(i == 0)
    def _():
        for h in range(CH_HEADS):
            rows = jnp.broadcast_to(diag_ref[h:h + 1, :], (CHUNK, diag_ref.shape[1]))
            bias_ref[h] = pltpu.roll(rows, 0, 1, stride=1, stride_axis=0)[:, :CH_BAND]

    lane = lax.broadcasted_iota(jnp.int32, (CHUNK, LANES), 1)
    key_col = lax.broadcasted_iota(jnp.int32, (CHUNK, CH_BAND), 1)
    blocks = [(c, g) for c in range(chunks) for g in range(groups)]
    starts = [pl.multiple_of((i * chunks + c) * CHUNK, CHUNK) for c in range(chunks)]

    scores = []
    for c, g in blocks:
        gl = slice(g * LANES, (g + 1) * LANES)
        q = q_ref[0, c * CHUNK:(c + 1) * CHUNK, gl]
        qq = jnp.concatenate(
            [jnp.where((lane >= hh * CH_HD) & (lane < (hh + 1) * CH_HD), q, jnp.zeros_like(q))
             for hh in range(heads_per_group)], axis=0)
        scores.append(_dot_nt(qq, k_ref[0, pl.ds(starts[c], CH_BAND), gl]))

    probs, denoms = [], []
    for (c, g), s in zip(blocks, scores):
        valid = key_col >= CH_PAD - starts[c]
        s = jnp.concatenate(
            [jnp.where(valid, s[hh * CHUNK:(hh + 1) * CHUNK] + bias_ref[g * heads_per_group + hh],
                       -jnp.inf) for hh in range(heads_per_group)], axis=0)
        p = jnp.exp2(s - jnp.max(s, axis=1, keepdims=True))
        denoms.append(jnp.sum(p, axis=1, keepdims=True))
        probs.append(p.astype(BF16))

    for (c, g), p, l in zip(blocks, probs, denoms):
        gl = slice(g * LANES, (g + 1) * LANES)
        o = _dot(p, v_ref[0, pl.ds(starts[c], CH_BAND), gl]) / l
        o_ref[0, c * CHUNK:(c + 1) * CHUNK, gl] = jnp.where(lane < CH_HD, o[:CHUNK], o[CHUNK:])


def _chunk_attention(qc, kc_pad, vc_pad, diag):
    B, S, _ = qc.shape
    return pl.pallas_call(
        _chunk_kernel,
        grid=(B, S // CH_ROWS),
        in_specs=[
            pl.BlockSpec((1, CH_ROWS, CH_WIDTH), lambda b, i: (b, i, 0)),
            pl.BlockSpec((1, S + CH_PAD, CH_WIDTH), lambda b, i: (b, 0, 0)),
            pl.BlockSpec((1, S + CH_PAD, CH_WIDTH), lambda b, i: (b, 0, 0)),
            pl.BlockSpec(diag.shape, lambda b, i: (0, 0)),
        ],
        out_specs=pl.BlockSpec((1, CH_ROWS, CH_WIDTH), lambda b, i: (b, i, 0)),
        out_shape=jax.ShapeDtypeStruct((B, S, CH_WIDTH), F32),
        scratch_shapes=[pltpu.VMEM((CH_HEADS, CHUNK, CH_BAND), F32)],
        compiler_params=pltpu.CompilerParams(
            dimension_semantics=("parallel", "arbitrary"), vmem_limit_bytes=VMEM_LIMIT),
        name="chunk_attn",
    )(qc, kc_pad, vc_pad, diag)


def _outproj_kernel(omt_ref, odt_ref, oc_ref, x_ref, gm_ref, gc_ref, w_ref, o_ref):
    om = _rms(omt_ref[0], gm_ref[...], axis=0).T.astype(BF16)
    od = odt_ref[0].T.astype(BF16)
    oc = _rms(oc_ref[0], gc_ref[...]).astype(BF16)
    mix = jnp.concatenate([om, od, oc], axis=1)
    o_ref[0] = x_ref[0] + _dot(mix, w_ref[...])


def _outproj(omt, odt, oc, x, gm, gc, w):
    B, S, D = x.shape
    tm = min(ROW_TILE, S)
    full = lambda a: pl.BlockSpec(a.shape, lambda b, i: (0,) * a.ndim)
    return pl.pallas_call(
        _outproj_kernel,
        grid=(B, S // tm),
        in_specs=[
            pl.BlockSpec((1, MLA_WIDTH, tm), lambda b, i: (b, 0, i)),
            pl.BlockSpec((1, DIFF_WIDTH, tm), lambda b, i: (b, 0, i)),
            pl.BlockSpec((1, tm, CH_WIDTH), lambda b, i: (b, i, 0)),
            pl.BlockSpec((1, tm, D), lambda b, i: (b, i, 0)),
            full(gm), full(gc), full(w),
        ],
        out_specs=pl.BlockSpec((1, tm, D), lambda b, i: (b, i, 0)),
        out_shape=jax.ShapeDtypeStruct((B, S, D), F32),
        compiler_params=pltpu.CompilerParams(
            dimension_semantics=("parallel", "parallel"), vmem_limit_bytes=VMEM_LIMIT),
        name="outproj",
    )(omt, odt, oc, x, gm, gc, w)


HALO = 8


def _ffn_kernel(x_ref, g_ref, wi_ref, cw_ref, cb_ref, wo_ref, gf_ref, o_ref, h_ref, carry_ref,
                *, final_norm):
    i = pl.program_id(1)
    tm = x_ref.shape[1]

    @pl.when(i == 0)
    def _():
        carry_ref[...] = jnp.zeros(carry_ref.shape, F32)

    x = x_ref[0]
    xn = _rms(x, g_ref[...]).astype(BF16)

    def project(c):
        lo, w = FF_CHUNKS[c]
        for half in range(2):
            col = half * D_FF + lo
            h_ref[c % 2, half, HALO:HALO + tm, 0:w] = _dot(xn, wi_ref[:, col:col + w])

    def gate(c):
        lo, w = FF_CHUNKS[c]
        halves = []
        for half in range(2):
            col = half * D_FF + lo
            h = h_ref.at[c % 2, half]
            h[0:HALO, 0:w] = carry_ref[half, :, lo:lo + w]
            carry_ref[half, :, lo:lo + w] = h[tm:tm + HALO, 0:w]
            cw = cw_ref[:, col:col + w]
            halves.append(cw[0:1] * h[HALO - 2:HALO - 2 + tm, 0:w]
                          + cw[1:2] * h[HALO - 1:HALO - 1 + tm, 0:w]
                          + cw[2:3] * h[HALO:HALO + tm, 0:w] + cb_ref[:, col:col + w])
        u, g = halves
        return (g / (1.0 + jnp.exp(-g)) * u).astype(BF16)

    project(0)
    y = x
    for c, (lo, w) in enumerate(FF_CHUNKS):
        if c + 1 < len(FF_CHUNKS):
            project(c + 1)
        y = y + _dot(gate(c), wo_ref[lo:lo + w, :])
    if final_norm:
        y = _rms(y, gf_ref[...])
    o_ref[0] = y


def _ffn(x, g, w_in, conv_w, conv_b, w_out, g_final, final_norm):
    B, S, D = x.shape
    tm = min(ROW_TILE, S)
    resident = lambda a: pl.BlockSpec(a.shape, lambda b, i: (0,) * a.ndim,
                                      pipeline_mode=pl.Buffered(1))
    chunk_w = max(w for _, w in FF_CHUNKS)
    return pl.pallas_call(
        functools.partial(_ffn_kernel, final_norm=final_norm),
        grid=(B, S // tm),
        in_specs=[
            pl.BlockSpec((1, tm, D), lambda b, i: (b, i, 0)),
            resident(g), resident(w_in), resident(conv_w), resident(conv_b), resident(w_out),
            resident(g_final),
        ],
        out_specs=pl.BlockSpec((1, tm, D), lambda b, i: (b, i, 0)),
        out_shape=jax.ShapeDtypeStruct((B, S, D), F32),
        scratch_shapes=[
            pltpu.VMEM((2, 2, HALO + tm, chunk_w), F32),
            pltpu.VMEM((2, HALO, D_FF), F32),
        ],
        compiler_params=pltpu.CompilerParams(
            dimension_semantics=("parallel", "arbitrary"), vmem_limit_bytes=VMEM_LIMIT),
        name="ffn",
    )(x, g, w_in, conv_w, conv_b, w_out, g_final)


def _rotate_half_cols(w):
    half = w.shape[1] // 2
    return jnp.concatenate([-w[:, half:], w[:, :half]], axis=1)


def _place(cols, offset, width=LANES):
    return jnp.pad(cols, ((0, 0), (offset, width - offset - cols.shape[1])))


def _layout_w_in(w):
    kr = w[:, MLA_Q_RANK + MLA_KV_RANK:IN_MLA]
    return jnp.concatenate([
        w[:, :MLA_Q_RANK + MLA_KV_RANK],
        _place(kr, MLA_NOPE), _place(_rotate_half_cols(kr), MLA_NOPE),
        w[:, IN_MLA:],
    ], axis=1).astype(BF16)


def _layout_w_uq(w):
    hd = MLA_NOPE + MLA_ROPE
    plain, rot = [], []
    for h in range(MLA_HEADS):
        wh = w[:, h * hd:(h + 1) * hd]
        plain.append(_place(wh, 0))
        rot.append(_place(_rotate_half_cols(wh[:, MLA_NOPE:]), MLA_NOPE))
    return jnp.concatenate(plain + rot, axis=1).astype(BF16)


def _layout_w_ukv(w):
    hd = MLA_NOPE + MLA_V
    ks = [_place(w[:, h * hd:h * hd + MLA_NOPE], 0) for h in range(MLA_HEADS)]
    vs = [w[:, h * hd + MLA_NOPE:(h + 1) * hd] for h in range(MLA_HEADS)]
    return jnp.concatenate(ks + vs, axis=1).astype(BF16)


def _slope_columns(slopes2):
    p1 = slopes2.astype(BF16).astype(F32)
    p2 = (slopes2 - p1).astype(BF16).astype(F32)
    p3 = (slopes2 - p1 - p2).astype(BF16).astype(F32)
    pieces = jnp.stack([p1, p2, p3], axis=1)
    cols = jnp.concatenate([pieces * 65536.0, pieces * 256.0, pieces], axis=1)
    return _place(cols, POS_COL0).reshape(1, -1)


def _ordered_query_tiles(rel):
    B, S = rel.shape
    tiles = rel.reshape(B, S // TQ, TQ)
    tile_max, tile_min = jnp.max(tiles, axis=2), jnp.min(tiles, axis=2)
    before = lax.cummax(tile_max, axis=1)
    before = jnp.concatenate([tile_min[:, :1], before[:, :-1]], axis=1)
    return (before <= tile_min).astype(jnp.int32)


def _chunk_bias_diagonals(rel_bias):
    n_rel = rel_bias.shape[1]
    far = jnp.broadcast_to(rel_bias[:, n_rel - 1:], (rel_bias.shape[0], REL_MAX))
    wrap = far[:, :CHUNK]
    return jnp.concatenate([far, rel_bias[:, ::-1], wrap], axis=1) * LOG2E


def kernel(x, positions, attn_norm, w_in, mla_q_norm, mla_w_uq, mla_kv_norm, mla_w_ukv, diff_lambda, diff_norm, chunk_rel_bias, mla_out_norm, chunk_out_norm, w_out, ffn_norm, w_ffn_in, ffn_conv_w, ffn_conv_b, w_ffn_out, final_norm):
    B, S, D = x.shape
    pos_col = positions.astype(F32)[:, :, None]
    rel = positions - positions[:, :1]
    rel_f = rel.astype(F32)
    half = MLA_ROPE // 2
    inv = ROPE_THETA ** (-jnp.arange(half, dtype=F32) / half)
    inv_slot = _place(jnp.concatenate([inv, inv])[None, :], MLA_NOPE)
    slopes = 2.0 ** (-8.0 * jnp.arange(1, DIFF_HEADS + 1, dtype=F32) / DIFF_HEADS)
    slopes2 = slopes * LOG2E
    ordered = _ordered_query_tiles(rel)
    slope_cols = _slope_columns(slopes2)
    slopes2 = slopes2.reshape(DIFF_HEADS, 1, 1)
    row = lambda v: v.reshape(1, -1)

    for l in range(DEPTH):
        lam_init = 0.8 - 0.6 * math.exp(-0.3 * l)
        qm, km, vmt, qd, kd, vdt, qc, kc, vc = _inproj(
            x, pos_col, rel[:, :, None], row(attn_norm[l]), _layout_w_in(w_in[l]),
            row(mla_q_norm[l]), _layout_w_uq(mla_w_uq[l]), row(mla_kv_norm[l]),
            _layout_w_ukv(mla_w_ukv[l]), inv_slot, slope_cols)
        omt = _mla_attention(qm, km, vmt)
        odt = _diff_attention(ordered, qd, kd, vdt, rel_f[:, None, :], rel_f[:, :, None], slopes2,
                              diff_lambda[l], diff_norm[l].reshape(DIFF_HEADS, DIFF_V, 1), lam_init)
        pad = ((0, 0), (CH_PAD, 0), (0, 0))
        oc = _chunk_attention(qc, jnp.pad(kc, pad), jnp.pad(vc, pad),
                              _chunk_bias_diagonals(chunk_rel_bias[l]))
        x = _outproj(omt, odt, oc, x, mla_out_norm[l].reshape(-1, 1), row(chunk_out_norm[l]),
                     w_out[l].astype(BF16))
        x = _ffn(x, row(ffn_norm[l]), w_ffn_in[l].astype(BF16), ffn_conv_w[l],
                 row(ffn_conv_b[l]), w_ffn_out[l].astype(BF16), row(final_norm),
                 final_norm=(l == DEPTH - 1))
    return x
```

```python
import functools
import math

import jax
import jax.numpy as jnp
from jax import lax
from jax.experimental import pallas as pl
from jax.experimental.pallas import tpu as pltpu

F32 = jnp.float32
BF16 = jnp.bfloat16

D_MODEL = 1024
DEPTH = 2
CHUNK = 64
EPS = 1e-6

MLA_HEADS = 6
MLA_Q_RANK = 256
MLA_KV_RANK = 128
MLA_NOPE = 64
MLA_ROPE = 32
MLA_V = 64
ROPE_THETA = 10000.0

DIFF_HEADS = 6
DIFF_QK = 32
DIFF_V = 64

CH_HEADS = 4
CH_HD = 64
CH_LEFT = 8
CH_BAND = (CH_LEFT + 1) * CHUNK
REL_MAX = 256

MLA_WIDTH = MLA_HEADS * MLA_V
DIFF_WIDTH = DIFF_HEADS * DIFF_V
CH_WIDTH = CH_HEADS * CH_HD
DIFF_QK_WIDTH = DIFF_HEADS * 2 * DIFF_QK

IN_MLA = MLA_Q_RANK + MLA_KV_RANK + MLA_ROPE
D_FF = 2816

LOG2E = math.log2(math.e)

LANES = 128

ROW_TILE = 512
TK = 512
MLA_TQ = 512
DIFF_TQ = 256
FF_CHUNKS = ((0, 1024), (1024, 1024), (2048, 768))
VT_ROWS = 80
MLA_HEADS_PER_STEP = 6
DIFF_HEADS_PER_STEP = 6
POS_COL0 = 2 * DIFF_QK
VMEM_LIMIT = 56 * 1024 * 1024

P_CQ = 0
P_CKV = MLA_Q_RANK
P_KR = P_CKV + MLA_KV_RANK
P_KRR = P_KR + LANES
P_DQ = P_KRR + LANES
P_DK = P_DQ + DIFF_QK_WIDTH
P_DV = P_DK + DIFF_QK_WIDTH
P_CQ_ = P_DV + DIFF_WIDTH
P_CK_ = P_CQ_ + CH_WIDTH
P_CV_ = P_CK_ + CH_WIDTH
P_TOTAL = P_CV_ + CH_WIDTH


def _rms(x, g, axis=-1):
    return x * lax.rsqrt(jnp.mean(x * x, axis=axis, keepdims=True) + EPS) * g


def _dot(a, b):
    return jnp.dot(a, b, preferred_element_type=F32)


def _dot_nt(a, b):
    return lax.dot_general(a, b, (((1,), (1,)), ((), ())), preferred_element_type=F32)


def _inproj_kernel(x_ref, pos_ref, rel_ref, g_ref, win_ref, qn_ref, wq_ref, kvn_ref, wkv_ref,
                   inv_ref, qcols_ref, qm_ref, km_ref, vmt_ref, qd_ref, kd_ref, vdt_ref, qc_ref, kc_ref, vc_ref):
    tm = x_ref.shape[1]
    xn = _rms(x_ref[0], g_ref[...]).astype(BF16)
    proj = _dot(xn, win_ref[...])

    ang = pos_ref[0] * inv_ref[...]
    cos = jnp.cos(ang)
    sin = jnp.sin(ang)

    cqn = _rms(proj[:, P_CQ:P_CKV], qn_ref[...]).astype(BF16)
    q2 = _dot(cqn, wq_ref[...])
    ckvn = _rms(proj[:, P_CKV:P_KR], kvn_ref[...]).astype(BF16)
    kv2 = _dot(ckvn, wkv_ref[...])

    k_rope = proj[:, P_KR:P_KRR] * cos + proj[:, P_KRR:P_DQ] * sin
    q_scale = (MLA_NOPE + MLA_ROPE) ** -0.5 * LOG2E
    nh = MLA_HEADS * LANES
    q_heads = []
    for h in range(MLA_HEADS):
        sl = slice(h * LANES, (h + 1) * LANES)
        qh = q2[:, sl] * cos + q2[:, nh + h * LANES: nh + (h + 1) * LANES] * sin
        q_heads.append(qh * q_scale)
        km_ref[0, :, sl] = (kv2[:, sl] + k_rope).astype(BF16)
    qm_ref[0] = jnp.concatenate(q_heads, axis=1).T.astype(BF16)

    vm_t = kv2[:, nh:nh + MLA_WIDTH].T.astype(BF16)
    vd_t = proj[:, P_DV:P_CQ_].T.astype(BF16)
    ones_rows = jnp.where(lax.broadcasted_iota(jnp.int32, (VT_ROWS - MLA_V, TK), 0) == 0,
                          1.0, 0.0).astype(BF16)
    for h in range(MLA_HEADS):
        for t in range(tm // TK):
            vmt_ref[0, h, t, 0:MLA_V, :] = vm_t[h * MLA_V:(h + 1) * MLA_V, t * TK:(t + 1) * TK]
            vdt_ref[0, h, t, 0:DIFF_V, :] = vd_t[h * DIFF_V:(h + 1) * DIFF_V, t * TK:(t + 1) * TK]
            vmt_ref[0, h, t, MLA_V:VT_ROWS, :] = ones_rows
            vdt_ref[0, h, t, DIFF_V:VT_ROWS, :] = ones_rows

    lane = lax.broadcasted_iota(jnp.int32, (1, LANES), 1)
    rel = rel_ref[0]
    digits = [(rel >> 16).astype(F32), ((rel >> 8) & 255).astype(F32), (rel & 255).astype(F32)]
    in_cols = (lane >= POS_COL0) & (lane < POS_COL0 + 9)
    k_cols = jnp.where(in_cols, jnp.where(lane < POS_COL0 + 3, digits[0],
                                          jnp.where(lane < POS_COL0 + 6, digits[1], digits[2])), 0.0)
    qd = proj[:, P_DQ:P_DK] * (DIFF_QK ** -0.5 * LOG2E)
    kd = proj[:, P_DK:P_DV]
    heads_per_group = LANES // (2 * DIFF_QK)
    q_heads = []
    for h in range(DIFF_HEADS):
        gl = slice((h // heads_per_group) * LANES, (h // heads_per_group + 1) * LANES)
        sl = slice(h * LANES, (h + 1) * LANES)
        qh, kh = qd[:, gl], kd[:, gl]
        if h % heads_per_group:
            qh, kh = pltpu.roll(qh, 2 * DIFF_QK, 1), pltpu.roll(kh, 2 * DIFF_QK, 1)
        q_heads.append(jnp.where(lane < 2 * DIFF_QK, qh, qcols_ref[:, sl]))
        kd_ref[0, :, sl] = jnp.where(lane < 2 * DIFF_QK, kh, k_cols).astype(BF16)
    qd_ref[0] = jnp.concatenate(q_heads, axis=1).T.astype(BF16)
    qc_ref[0] = (proj[:, P_CQ_:P_CK_] * (CH_HD ** -0.5 * LOG2E)).astype(BF16)
    kc_ref[0] = proj[:, P_CK_:P_CV_].astype(BF16)
    vc_ref[0] = proj[:, P_CV_:P_TOTAL].astype(BF16)


def _inproj(x, pos_col, rel_col, g, win, qn, wq, kvn, wkv, inv, qcols):
    B, S, D = x.shape
    tm = min(ROW_TILE, S)
    nt = S // TK
    full = lambda a: pl.BlockSpec(a.shape, lambda b, i: (0,) * a.ndim)
    row = lambda w: pl.BlockSpec((1, tm, w), lambda b, i: (b, i, 0))
    col = lambda w: pl.BlockSpec((1, w, tm), lambda b, i: (b, 0, i))
    vt_spec =pl.BlockSpec((1, MLA_HEADS, tm // TK, VT_ROWS, TK), lambda b, i: (b, 0, i, 0, 0))
    out_shape = (
        jax.ShapeDtypeStruct((B, MLA_HEADS * LANES, S), BF16),
        jax.ShapeDtypeStruct((B, S, MLA_HEADS * LANES), BF16),
        jax.ShapeDtypeStruct((B, MLA_HEADS, nt, VT_ROWS, TK), BF16),
        jax.ShapeDtypeStruct((B, DIFF_HEADS * LANES, S), BF16),
        jax.ShapeDtypeStruct((B, S, DIFF_HEADS * LANES), BF16),
        jax.ShapeDtypeStruct((B, DIFF_HEADS, nt, VT_ROWS, TK), BF16),
        jax.ShapeDtypeStruct((B, S, CH_WIDTH), BF16),
        jax.ShapeDtypeStruct((B, S, CH_WIDTH), BF16),
        jax.ShapeDtypeStruct((B, S, CH_WIDTH), BF16),
    )
    return pl.pallas_call(
        _inproj_kernel,
        grid=(B, S // tm),
        in_specs=[row(D), row(1), row(1), full(g), full(win), full(qn), full(wq), full(kvn),
                  full(wkv), full(inv), full(qcols)],
        out_specs=(col(MLA_HEADS * LANES), row(MLA_HEADS * LANES), vt_spec,
                   col(DIFF_HEADS * LANES), row(DIFF_HEADS * LANES), vt_spec,
                   row(CH_WIDTH), row(CH_WIDTH), row(CH_WIDTH)),
        out_shape=out_shape,
        compiler_params=pltpu.CompilerParams(
            dimension_semantics=("parallel", "parallel"), vmem_limit_bytes=VMEM_LIMIT),
        name="inproj",
    )(x, pos_col, rel_col, g, win, qn, wq, kvn, wkv, inv, qcols)


def _chunk_causal_tile_mask(n_q, tq, q_offset):
    r = lax.broadcasted_iota(jnp.int32, (TK, n_q), 0) // CHUNK
    c = (lax.broadcasted_iota(jnp.int32, (TK, n_q), 1) % tq + q_offset) // CHUNK
    return r <= c


def _store_scores(s, s_ref, smax_ref, slot, h, diag):
    if diag is not None:
        s = jnp.where(_chunk_causal_tile_mask(s.shape[1], *diag), s, -jnp.inf)
    s_ref[slot, h] = s
    smax_ref[slot, h] = jnp.max(s, axis=0, keepdims=True)


def _softmax_tile_update(vt, s_ref, smax_ref, m_ref, acc_ref, slot, h):
    m = m_ref[h]
    m_new = jnp.maximum(m, smax_ref[slot, h])
    p = jnp.exp2(s_ref[slot, h] - m_new).astype(BF16)
    acc_ref[h] = jnp.exp2(m - m_new) * acc_ref[h] + _dot(vt, p)
    m_ref[h] = m_new


def _softmax_state_init(m_ref, acc_ref):
    m_ref[...] = jnp.full(m_ref.shape, -jnp.inf, F32)
    acc_ref[...] = jnp.zeros(acc_ref.shape, F32)


def _softmax_result(acc_ref, h, dv):
    acc = acc_ref[h]
    return acc[0:dv] / acc[dv:dv + 1]


def _pipelined_key_sweep(n, heads, tile_context, score_head, consume_head):
    def step(score=None, consume=None):
        ctx = tile_context(score[0], score[2]) if score else None
        for h in range(heads):
            if score:
                score_head(*score, h, ctx)
            if consume:
                consume_head(*consume, h)

    @pl.when(n == 0)
    def _():
        step(score=(0, 0, True))
        step(consume=(0, 0))

    @pl.when(n > 0)
    def _():
        step(score=(0, 0, False))

        def body(jj, carry):
            j = 2 * jj
            step(score=(j + 1, 1, False), consume=(j, 0))
            step(score=(j + 2, 0, False), consume=(j + 1, 1))
            return carry

        lax.fori_loop(0, (n - 1) // 2, body, 0)

        @pl.when(n % 2 == 1)
        def _():
            step(score=(n, 1, True), consume=(n - 1, 0))
            step(consume=(n, 1))

        @pl.when(n % 2 == 0)
        def _():
            step(score=(n - 1, 1, False), consume=(n - 2, 0))
            step(score=(n, 0, True), consume=(n - 1, 1))
            step(consume=(n, 0))


def _softmax_scratch(heads, n):
    return [pltpu.VMEM((2, heads, TK, n), F32), pltpu.VMEM((2, heads, 1, n), F32),
            pltpu.VMEM((heads, 1, n), F32), pltpu.VMEM((heads, VT_ROWS, n), F32)]


def _mla_kernel(q_ref, k_ref, vt_ref, o_ref, s_ref, smax_ref, m_ref, acc_ref):
    i = pl.program_id(2)
    heads = m_ref.shape[0]
    _softmax_state_init(m_ref, acc_ref)

    def tile_context(j, diag):
        return pl.multiple_of(j * TK, TK)

    def score_head(j, slot, diag, h, start):
        sl = slice(h * LANES, (h + 1) * LANES)
        s = _dot(k_ref[0, pl.ds(start, TK), sl], q_ref[0, sl, :])
        _store_scores(s, s_ref, smax_ref, slot, h, (MLA_TQ, 0) if diag else None)

    def consume_head(j, slot, h):
        _softmax_tile_update(vt_ref[0, h, j], s_ref, smax_ref, m_ref, acc_ref, slot, h)

    assert MLA_TQ == TK
    _pipelined_key_sweep(i, heads, tile_context, score_head, consume_head)
    for h in range(heads):
        o_ref[0, h * MLA_V:(h + 1) * MLA_V, :] = _softmax_result(acc_ref, h, MLA_V)


def _mla_attention(qm, km, vmt):
    B, S, _ = km.shape
    nt = S // TK
    hb = MLA_HEADS_PER_STEP
    once = pl.Buffered(1)
    return pl.pallas_call(
        _mla_kernel,
        grid=(B, MLA_HEADS // hb, S // MLA_TQ),
        in_specs=[
            pl.BlockSpec((1, hb * LANES, MLA_TQ), lambda b, h, i: (b, h, i)),
            pl.BlockSpec((1, S, hb * LANES), lambda b, h, i: (b, 0, h), pipeline_mode=once),
            pl.BlockSpec((1, hb, nt, VT_ROWS, TK), lambda b, h, i: (b, h, 0, 0, 0),
                         pipeline_mode=once),
        ],
        out_specs=pl.BlockSpec((1, hb * MLA_V, MLA_TQ), lambda b, h, i: (b, h, i)),
        out_shape=jax.ShapeDtypeStruct((B, MLA_WIDTH, S), F32),
        scratch_shapes=_softmax_scratch(hb, MLA_TQ),
        compiler_params=pltpu.CompilerParams(
            dimension_semantics=("parallel", "parallel", "arbitrary"),
            vmem_limit_bytes=VMEM_LIMIT),
        name="mla_attn",
    )(qm, km, vmt)


def _diff_kernel(ordered_ref, q_ref, k_ref, vt_ref, posq_ref, posk_ref, slope_ref, lam_ref, g_ref,
                 o_ref, qq_ref, s_ref, smax_ref, m_ref, acc_ref, *, lam_init):
    b = pl.program_id(0)
    i = pl.program_id(2)
    heads = m_ref.shape[0]
    _softmax_state_init(m_ref, acc_ref)
    dim = lax.broadcasted_iota(jnp.int32, (LANES, DIFF_TQ), 0)
    pos_q = posq_ref[0]
    q_tiles_per_key_tile = TK // DIFF_TQ
    diag_tile = i // q_tiles_per_key_tile
    q_offset = (i % q_tiles_per_key_tile) * DIFF_TQ

    def stack_queries(with_pos_cols):
        for h in range(heads):
            q = q_ref[0, h * LANES:(h + 1) * LANES, :]
            for mp in range(2):
                keep = (dim >= mp * DIFF_QK) & (dim < (mp + 1) * DIFF_QK)
                if with_pos_cols:
                    keep = keep | (dim >= POS_COL0)
                qq_ref[h, :, mp * DIFF_TQ:(mp + 1) * DIFF_TQ] = jnp.where(keep, q, jnp.zeros_like(q))

    def make_scores(ordered):
        def tile_context(j, diag):
            start = pl.multiple_of(j * TK, TK)
            if not (diag or not ordered):
                return start, None
            d = posk_ref[0, pl.ds(start, TK), :] - pos_q
            return start, (2.0 * jnp.maximum(d, 0.0) if ordered else jnp.abs(d))

        def score_head(j, slot, diag, h, ctx):
            start, dist = ctx
            s = _dot(k_ref[0, pl.ds(start, TK), h * LANES:(h + 1) * LANES], qq_ref[h])
            if dist is not None:
                bias = dist * slope_ref[h]
                s = s - jnp.concatenate([bias, bias], axis=1)
            _store_scores(s, s_ref, smax_ref, slot, h, (DIFF_TQ, q_offset) if diag else None)
        return tile_context, score_head

    def consume_head(j, slot, h):
        _softmax_tile_update(vt_ref[0, h, j], s_ref, smax_ref, m_ref, acc_ref, slot, h)

    @pl.when(ordered_ref[b, i] != 0)
    def _():
        stack_queries(True)
        _pipelined_key_sweep(diag_tile, heads, *make_scores(True), consume_head)

    @pl.when(ordered_ref[b, i] == 0)
    def _():
        stack_queries(False)
        _pipelined_key_sweep(diag_tile, heads, *make_scores(False), consume_head)

    lam = lam_ref[...]
    lam_full = (jnp.exp(jnp.sum(lam[0:1] * lam[1:2], keepdims=True))
                - jnp.exp(jnp.sum(lam[2:3] * lam[3:4], keepdims=True)) + lam_init)
    for h in range(heads):
        o = _softmax_result(acc_ref, h, DIFF_V)
        od = o[:, :DIFF_TQ] - lam_full * o[:, DIFF_TQ:]
        o_ref[0, h * DIFF_V:(h + 1) * DIFF_V, :] = _rms(od, g_ref[h], axis=0) * (1.0 - lam_init)


def _diff_attention(ordered, qd, kd, vdt, rel_row, rel_col, slopes, lam, sub_norm, lam_init):
    B, S, _ = kd.shape
    nt = S // TK
    hb = DIFF_HEADS_PER_STEP
    once = pl.Buffered(1)
    return pl.pallas_call(
        functools.partial(_diff_kernel, lam_init=lam_init),
        grid=(B, DIFF_HEADS // hb, S // DIFF_TQ),
        in_specs=[
            pl.BlockSpec(memory_space=pltpu.SMEM),
            pl.BlockSpec((1, hb * LANES, DIFF_TQ), lambda b, h, i: (b, h, i)),
            pl.BlockSpec((1, S, hb * LANES), lambda b, h, i: (b, 0, h), pipeline_mode=once),
            pl.BlockSpec((1, hb, nt, VT_ROWS, TK), lambda b, h, i: (b, h, 0, 0, 0),
                         pipeline_mode=once),
            pl.BlockSpec((1, 1, DIFF_TQ), lambda b, h, i: (b, 0, i)),
            pl.BlockSpec((1, S, 1), lambda b, h, i: (b, 0, 0), pipeline_mode=once),
            pl.BlockSpec((hb, 1, 1), lambda b, h, i: (h, 0, 0)),
            pl.BlockSpec(lam.shape, lambda b, h, i: (0, 0)),
            pl.BlockSpec((hb, DIFF_V, 1), lambda b, h, i: (h, 0, 0)),
        ],
        out_specs=pl.BlockSpec((1, hb * DIFF_V, DIFF_TQ), lambda b, h, i: (b, h, i)),
        out_shape=jax.ShapeDtypeStruct((B, DIFF_WIDTH, S), F32),
        scratch_shapes=([pltpu.VMEM((hb, LANES, 2 * DIFF_TQ), BF16)]
                        + _softmax_scratch(hb, 2 * DIFF_TQ)),
        compiler_params=pltpu.CompilerParams(
            dimension_semantics=("parallel", "parallel", "arbitrary"),
            vmem_limit_bytes=VMEM_LIMIT),
        name="diff_attn",
    )(ordered, qd, kd, vdt, rel_row, rel_col, slopes, lam, sub_norm)


CH_PAD = CH_LEFT * CHUNK
CH_ROWS = 256


def _chunk_kernel(q_ref, k_ref, v_ref, diag_ref, o_ref, bias_ref):
    i = pl.program_id(1)
    heads_per_group = LANES // CH_HD
    groups = CH_WIDTH // LANES
    chunks = CH_ROWS // CHUNK

    @pl.when(i == 0)
    def _():
        for h in range(CH_HEADS):
            rows = jnp.broadcast_to(diag_ref[h:h + 1, :], (CHUNK, diag_ref.shape[1]))
            bias_ref[h] = pltpu.roll(rows, 0, 1, stride=1, stride_axis=0)[:, :CH_BAND]

    lane = lax.broadcasted_iota(jnp.int32, (CHUNK, LANES), 1)
    key_col = lax.broadcasted_iota(jnp.int32, (CHUNK, CH_BAND), 1)
    blocks = [(c, g) for c in range(chunks) for g in range(groups)]
    starts = [pl.multiple_of((i * chunks + c) * CHUNK, CHUNK) for c in range(chunks)]

    scores = []
    for c, g in blocks:
        gl = slice(g * LANES, (g + 1) * LANES)
        q = q_ref[0, c * CHUNK:(c + 1) * CHUNK, gl]
        qq = jnp.concatenate(
            [jnp.where((lane >= hh * CH_HD) & (lane < (hh + 1) * CH_HD), q, jnp.zeros_like(q))
             for hh in range(heads_per_group)], axis=0)
        scores.append(_dot_nt(qq, k_ref[0, pl.ds(starts[c], CH_BAND), gl]))

    probs, denoms = [], []
    for (c, g), s in zip(blocks, scores):
        valid = key_col >= CH_PAD - starts[c]
        s = jnp.concatenate(
            [jnp.where(valid, s[hh * CHUNK:(hh + 1) * CHUNK] + bias_ref[g * heads_per_group + hh],
                       -jnp.inf) for hh in range(heads_per_group)], axis=0)
        p = jnp.exp2(s - jnp.max(s, axis=1, keepdims=True))
        denoms.append(jnp.sum(p, axis=1, keepdims=True))
        probs.append(p.astype(BF16))

    for (c, g), p, l in zip(blocks, probs, denoms):
        gl = slice(g * LANES, (g + 1) * LANES)
        o = _dot(p, v_ref[0, pl.ds(starts[c], CH_BAND), gl]) / l
        o_ref[0, c * CHUNK:(c + 1) * CHUNK, gl] = jnp.where(lane < CH_HD, o[:CHUNK], o[CHUNK:])


def _chunk_attention(qc, kc_pad, vc_pad, diag):
    B, S, _ = qc.shape
    return pl.pallas_call(
        _chunk_kernel,
        grid=(B, S // CH_ROWS),
        in_specs=[
            pl.BlockSpec((1, CH_ROWS, CH_WIDTH), lambda b, i: (b, i, 0)),
            pl.BlockSpec((1, S + CH_PAD, CH_WIDTH), lambda b, i: (b, 0, 0)),
            pl.BlockSpec((1, S + CH_PAD, CH_WIDTH), lambda b, i: (b, 0, 0)),
            pl.BlockSpec(diag.shape, lambda b, i: (0, 0)),
        ],
        out_specs=pl.BlockSpec((1, CH_ROWS, CH_WIDTH), lambda b, i: (b, i, 0)),
        out_shape=jax.ShapeDtypeStruct((B, S, CH_WIDTH), F32),
        scratch_shapes=[pltpu.VMEM((CH_HEADS, CHUNK, CH_BAND), F32)],
        compiler_params=pltpu.CompilerParams(
            dimension_semantics=("parallel", "arbitrary"), vmem_limit_bytes=VMEM_LIMIT),
        name="chunk_attn",
    )(qc, kc_pad, vc_pad, diag)


def _outproj_kernel(omt_ref, odt_ref, oc_ref, x_ref, gm_ref, gc_ref, w_ref, o_ref):
    om = _rms(omt_ref[0], gm_ref[...], axis=0).T.astype(BF16)
    od = odt_ref[0].T.astype(BF16)
    oc = _rms(oc_ref[0], gc_ref[...]).astype(BF16)
    mix = jnp.concatenate([om, od, oc], axis=1)
    o_ref[0] = x_ref[0] + _dot(mix, w_ref[...])


def _outproj(omt, odt, oc, x, gm, gc, w):
    B, S, D = x.shape
    tm = min(ROW_TILE, S)
    full = lambda a: pl.BlockSpec(a.shape, lambda b, i: (0,) * a.ndim)
    return pl.pallas_call(
        _outproj_kernel,
        grid=(B, S // tm),
        in_specs=[
            pl.BlockSpec((1, MLA_WIDTH, tm), lambda b, i: (b, 0, i)),
            pl.BlockSpec((1, DIFF_WIDTH, tm), lambda b, i: (b, 0, i)),
            pl.BlockSpec((1, tm, CH_WIDTH), lambda b, i: (b, i, 0)),
            pl.BlockSpec((1, tm, D), lambda b, i: (b, i, 0)),
            full(gm), full(gc), full(w),
        ],
        out_specs=pl.BlockSpec((1, tm, D), lambda b, i: (b, i, 0)),
        out_shape=jax.ShapeDtypeStruct((B, S, D), F32),
        compiler_params=pltpu.CompilerParams(
            dimension_semantics=("parallel", "parallel"), vmem_limit_bytes=VMEM_LIMIT),
        name="outproj",
    )(omt, odt, oc, x, gm, gc, w)


HALO = 8


def _ffn_kernel(x_ref, g_ref, wi_ref, cw_ref, cb_ref, wo_ref, gf_ref, o_ref, h_ref, carry_ref,
                *, final_norm):
    i = pl.program_id(1)
    tm = x_ref.shape[1]

    @pl.when(i == 0)
    def _():
        carry_ref[...] = jnp.zeros(carry_ref.shape, F32)

    x = x_ref[0]
    xn = _rms(x, g_ref[...]).astype(BF16)

    def project(c):
        lo, w = FF_CHUNKS[c]
        for half in range(2):
            col = half * D_FF + lo
            h_ref[c % 2, half, HALO:HALO + tm, 0:w] = _dot(xn, wi_ref[:, col:col + w])

    def gate(c):
        lo, w = FF_CHUNKS[c]
        halves = []
        for half in range(2):
            col = half * D_FF + lo
            h = h_ref.at[c % 2, half]
            h[0:HALO, 0:w] = carry_ref[half, :, lo:lo + w]
            carry_ref[half, :, lo:lo + w] = h[tm:tm + HALO, 0:w]
            cw = cw_ref[:, col:col + w]
            halves.append(cw[0:1] * h[HALO - 2:HALO - 2 + tm, 0:w]
                          + cw[1:2] * h[HALO - 1:HALO - 1 + tm, 0:w]
                          + cw[2:3] * h[HALO:HALO + tm, 0:w] + cb_ref[:, col:col + w])
        u, g = halves
        return (g / (1.0 + jnp.exp(-g)) * u).astype(BF16)

    project(0)
    y = x
    for c, (lo, w) in enumerate(FF_CHUNKS):
        if c + 1 < len(FF_CHUNKS):
            project(c + 1)
        y = y + _dot(gate(c), wo_ref[lo:lo + w, :])
    if final_norm:
        y = _rms(y, gf_ref[...])
    o_ref[0] = y


def _ffn(x, g, w_in, conv_w, conv_b, w_out, g_final, final_norm):
    B, S, D = x.shape
    tm = min(ROW_TILE, S)
    resident = lambda a: pl.BlockSpec(a.shape, lambda b, i: (0,) * a.ndim,
                                      pipeline_mode=pl.Buffered(1))
    chunk_w = max(w for _, w in FF_CHUNKS)
    return pl.pallas_call(
        functools.partial(_ffn_kernel, final_norm=final_norm),
        grid=(B, S // tm),
        in_specs=[
            pl.BlockSpec((1, tm, D), lambda b, i: (b, i, 0)),
            resident(g), resident(w_in), resident(conv_w), resident(conv_b), resident(w_out),
            resident(g_final),
        ],
        out_specs=pl.BlockSpec((1, tm, D), lambda b, i: (b, i, 0)),
        out_shape=jax.ShapeDtypeStruct((B, S, D), F32),
        scratch_shapes=[
            pltpu.VMEM((2, 2, HALO + tm, chunk_w), F32),
            pltpu.VMEM((2, HALO, D_FF), F32),
        ],
        compiler_params=pltpu.CompilerParams(
            dimension_semantics=("parallel", "arbitrary"), vmem_limit_bytes=VMEM_LIMIT),
        name="ffn",
    )(x, g, w_in, conv_w, conv_b, w_out, g_final)


def _rotate_half_cols(w):
    half = w.shape[1] // 2
    return jnp.concatenate([-w[:, half:], w[:, :half]], axis=1)


def _place(cols, offset, width=LANES):
    return jnp.pad(cols, ((0, 0), (offset, width - offset - cols.shape[1])))


def _layout_w_in(w):
    kr = w[:, MLA_Q_RANK + MLA_KV_RANK:IN_MLA]
    return jnp.concatenate([
        w[:, :MLA_Q_RANK + MLA_KV_RANK],
        _place(kr, MLA_NOPE), _place(_rotate_half_cols(kr), MLA_NOPE),
        w[:, IN_MLA:],
    ], axis=1).astype(BF16)


def _layout_w_uq(w):
    hd = MLA_NOPE + MLA_ROPE
    plain, rot = [], []
    for h in range(MLA_HEADS):
        wh = w[:, h * hd:(h + 1) * hd]
        plain.append(_place(wh, 0))
        rot.append(_place(_rotate_half_cols(wh[:, MLA_NOPE:]), MLA_NOPE))
    return jnp.concatenate(plain + rot, axis=1).astype(BF16)


def _layout_w_ukv(w):
    hd = MLA_NOPE + MLA_V
    ks = [_place(w[:, h * hd:h * hd + MLA_NOPE], 0) for h in range(MLA_HEADS)]
    vs = [w[:, h * hd + MLA_NOPE:(h + 1) * hd] for h in range(MLA_HEADS)]
    return jnp.concatenate(ks + vs, axis=1).astype(BF16)


def _slope_columns(slopes2):
    p1 = slopes2.astype(BF16).astype(F32)
    p2 = (slopes2 - p1).astype(BF16).astype(F32)
    p3 = (slopes2 - p1 - p2).astype(BF16).astype(F32)
    pieces = jnp.stack([p1, p2, p3], axis=1)
    cols = jnp.concatenate([pieces * 65536.0, pieces * 256.0, pieces], axis=1)
    return _place(cols, POS_COL0).reshape(1, -1)


def _ordered_query_tiles(rel):
    B, S = rel.shape
    query_min = jnp.min(rel.reshape(B, S // DIFF_TQ, DIFF_TQ), axis=2)
    key_max = lax.cummax(jnp.max(rel.reshape(B, S // TK, TK), axis=2), axis=1)
    lowest = jnp.full((B, 1), jnp.iinfo(rel.dtype).min, rel.dtype)
    before = jnp.concatenate([lowest, key_max[:, :-1]], axis=1)
    before = jnp.repeat(before, TK // DIFF_TQ, axis=1)
    return (before <= query_min).astype(jnp.int32)


def _chunk_bias_diagonals(rel_bias):
    n_rel = rel_bias.shape[1]
    far = jnp.broadcast_to(rel_bias[:, n_rel - 1:], (rel_bias.shape[0], REL_MAX))
    wrap = far[:, :CHUNK]
    return jnp.concatenate([far, rel_bias[:, ::-1], wrap], axis=1) * LOG2E


def kernel(x, positions, attn_norm, w_in, mla_q_norm, mla_w_uq, mla_kv_norm, mla_w_ukv, diff_lambda, diff_norm, chunk_rel_bias, mla_out_norm, chunk_out_norm, w_out, ffn_norm, w_ffn_in, ffn_conv_w, ffn_conv_b, w_ffn_out, final_norm):
    B, S, D = x.shape
    pos_col = positions.astype(F32)[:, :, None]
    rel = positions - positions[:, :1]
    rel_f = rel.astype(F32)
    half = MLA_ROPE // 2
    inv = ROPE_THETA ** (-jnp.arange(half, dtype=F32) / half)
    inv_slot = _place(jnp.concatenate([inv, inv])[None, :], MLA_NOPE)
    slopes = 2.0 ** (-8.0 * jnp.arange(1, DIFF_HEADS + 1, dtype=F32) / DIFF_HEADS)
    slopes2 = slopes * LOG2E
    ordered = _ordered_query_tiles(rel)
    slope_cols = _slope_columns(slopes2)
    slopes2 = slopes2.reshape(DIFF_HEADS, 1, 1)
    row = lambda v: v.reshape(1, -1)

    for l in range(DEPTH):
        lam_init = 0.8 - 0.6 * math.exp(-0.3 * l)
        qm, km, vmt, qd, kd, vdt, qc, kc, vc = _inproj(
            x, pos_col, rel[:, :, None], row(attn_norm[l]), _layout_w_in(w_in[l]),
            row(mla_q_norm[l]), _layout_w_uq(mla_w_uq[l]), row(mla_kv_norm[l]),
            _layout_w_ukv(mla_w_ukv[l]), inv_slot, slope_cols)
        omt = _mla_attention(qm, km, vmt)
        odt = _diff_attention(ordered, qd, kd, vdt, rel_f[:, None, :], rel_f[:, :, None], slopes2,
                              diff_lambda[l], diff_norm[l].reshape(DIFF_HEADS, DIFF_V, 1), lam_init)
        pad = ((0, 0), (CH_PAD, 0), (0, 0))
        oc = _chunk_attention(qc, jnp.pad(kc, pad), jnp.pad(vc, pad),
                              _chunk_bias_diagonals(chunk_rel_bias[l]))
        x = _outproj(omt, odt, oc, x, mla_out_norm[l].reshape(-1, 1), row(chunk_out_norm[l]),
                     w_out[l].astype(BF16))
        x = _ffn(x, row(ffn_norm[l]), w_ffn_in[l].astype(BF16), ffn_conv_w[l],
                 row(ffn_conv_b[l]), w_ffn_out[l].astype(BF16), row(final_norm),
                 final_norm=(l == DEPTH - 1))
    return x
```

```python
import functools
import math

import jax
import jax.numpy as jnp
from jax import lax
from jax.experimental import pallas as pl
from jax.experimental.pallas import tpu as pltpu

F32 = jnp.float32
BF16 = jnp.bfloat16

D_MODEL = 1024
DEPTH = 2
CHUNK = 64
EPS = 1e-6

MLA_HEADS = 6
MLA_Q_RANK = 256
MLA_KV_RANK = 128
MLA_NOPE = 64
MLA_ROPE = 32
MLA_V = 64
ROPE_THETA = 10000.0

DIFF_HEADS = 6
DIFF_QK = 32
DIFF_V = 64

CH_HEADS = 4
CH_HD = 64
CH_LEFT = 8
CH_BAND = (CH_LEFT + 1) * CHUNK
REL_MAX = 256

MLA_WIDTH = MLA_HEADS * MLA_V
DIFF_WIDTH = DIFF_HEADS * DIFF_V
CH_WIDTH = CH_HEADS * CH_HD
DIFF_QK_WIDTH = DIFF_HEADS * 2 * DIFF_QK

IN_MLA = MLA_Q_RANK + MLA_KV_RANK + MLA_ROPE
D_FF = 2816

LOG2E = math.log2(math.e)

LANES = 128

ROW_TILE = 512
TK = 512
MLA_TQ = 512
DIFF_TQ = 256
FF_CHUNKS = ((0, 1024), (1024, 1024), (2048, 768))
VT_ROWS = 80
MLA_HEADS_PER_STEP = 6
DIFF_HEADS_PER_STEP = 6
POS_COL0 = 2 * DIFF_QK
VMEM_LIMIT = 56 * 1024 * 1024

P_CQ = 0
P_CKV = MLA_Q_RANK
P_KR = P_CKV + MLA_KV_RANK
P_KRR = P_KR + LANES
P_DQ = P_KRR + LANES
P_DK = P_DQ + DIFF_QK_WIDTH
P_DV = P_DK + DIFF_QK_WIDTH
P_CQ_ = P_DV + DIFF_WIDTH
P_CK_ = P_CQ_ + CH_WIDTH
P_CV_ = P_CK_ + CH_WIDTH
P_TOTAL = P_CV_ + CH_WIDTH


def _rms(x, g, axis=-1):
    return x * lax.rsqrt(jnp.mean(x * x, axis=axis, keepdims=True) + EPS) * g


def _dot(a, b):
    return jnp.dot(a, b, preferred_element_type=F32)


def _dot_nt(a, b):
    return lax.dot_general(a, b, (((1,), (1,)), ((), ())), preferred_element_type=F32)


def _inproj_kernel(x_ref, pos_ref, rel_ref, g_ref, win_ref, qn_ref, wq_ref, kvn_ref, wkv_ref,
                   inv_ref, qcols_ref, qm_ref, km_ref, vmt_ref, qd_ref, kd_ref, vdt_ref, qc_ref, kc_ref, vc_ref):
    tm = x_ref.shape[1]
    xn = _rms(x_ref[0], g_ref[...]).astype(BF16)
    proj = _dot(xn, win_ref[...])

    ang = pos_ref[0] * inv_ref[...]
    cos = jnp.cos(ang)
    sin = jnp.sin(ang)

    cqn = _rms(proj[:, P_CQ:P_CKV], qn_ref[...]).astype(BF16)
    q2 = _dot(cqn, wq_ref[...])
    ckvn = _rms(proj[:, P_CKV:P_KR], kvn_ref[...]).astype(BF16)
    kv2 = _dot(ckvn, wkv_ref[...])

    k_rope = proj[:, P_KR:P_KRR] * cos + proj[:, P_KRR:P_DQ] * sin
    q_scale = (MLA_NOPE + MLA_ROPE) ** -0.5 * LOG2E
    nh = MLA_HEADS * LANES
    q_heads = []
    for h in range(MLA_HEADS):
        sl = slice(h * LANES, (h + 1) * LANES)
        qh = q2[:, sl] * cos + q2[:, nh + h * LANES: nh + (h + 1) * LANES] * sin
        q_heads.append(qh * q_scale)
        km_ref[0, :, sl] = (kv2[:, sl] + k_rope).astype(BF16)
    qm_ref[0] = jnp.concatenate(q_heads, axis=1).T.astype(BF16)

    vm_t = kv2[:, nh:nh + MLA_WIDTH].T.astype(BF16)
    vd_t = proj[:, P_DV:P_CQ_].T.astype(BF16)
    ones_rows = jnp.where(lax.broadcasted_iota(jnp.int32, (VT_ROWS - MLA_V, TK), 0) == 0,
                          1.0, 0.0).astype(BF16)
    for h in range(MLA_HEADS):
        for t in range(tm // TK):
            vmt_ref[0, h, t, 0:MLA_V, :] = vm_t[h * MLA_V:(h + 1) * MLA_V, t * TK:(t + 1) * TK]
            vdt_ref[0, h, t, 0:DIFF_V, :] = vd_t[h * DIFF_V:(h + 1) * DIFF_V, t * TK:(t + 1) * TK]
            vmt_ref[0, h, t, MLA_V:VT_ROWS, :] = ones_rows
            vdt_ref[0, h, t, DIFF_V:VT_ROWS, :] = ones_rows

    lane = lax.broadcasted_iota(jnp.int32, (1, LANES), 1)
    rel = rel_ref[0]
    digits = [(rel >> 16).astype(F32), ((rel >> 8) & 255).astype(F32), (rel & 255).astype(F32)]
    in_cols = (lane >= POS_COL0) & (lane < POS_COL0 + 9)
    k_cols = jnp.where(in_cols, jnp.where(lane < POS_COL0 + 3, digits[0],
                                          jnp.where(lane < POS_COL0 + 6, digits[1], digits[2])), 0.0)
    qd = proj[:, P_DQ:P_DK] * (DIFF_QK ** -0.5 * LOG2E)
    kd = proj[:, P_DK:P_DV]
    heads_per_group = LANES // (2 * DIFF_QK)
    q_heads = []
    for h in range(DIFF_HEADS):
        gl = slice((h // heads_per_group) * LANES, (h // heads_per_group + 1) * LANES)
        sl = slice(h * LANES, (h + 1) * LANES)
        qh, kh = qd[:, gl], kd[:, gl]
        if h % heads_per_group:
            qh, kh = pltpu.roll(qh, 2 * DIFF_QK, 1), pltpu.roll(kh, 2 * DIFF_QK, 1)
        q_heads.append(jnp.where(lane < 2 * DIFF_QK, qh, qcols_ref[:, sl]))
        kd_ref[0, :, sl] = jnp.where(lane < 2 * DIFF_QK, kh, k_cols).astype(BF16)
    qd_ref[0] = jnp.concatenate(q_heads, axis=1).T.astype(BF16)
    qc_ref[0] = (proj[:, P_CQ_:P_CK_] * (CH_HD ** -0.5 * LOG2E)).astype(BF16)
    kc_ref[0] = proj[:, P_CK_:P_CV_].astype(BF16)
    vc_ref[0] = proj[:, P_CV_:P_TOTAL].astype(BF16)


def _inproj(x, pos_col, rel_col, g, win, qn, wq, kvn, wkv, inv, qcols):
    B, S, D = x.shape
    tm = min(ROW_TILE, S)
    nt = S // TK
    full = lambda a: pl.BlockSpec(a.shape, lambda b, i: (0,) * a.ndim)
    row = lambda w: pl.BlockSpec((1, tm, w), lambda b, i: (b, i, 0))
    col = lambda w: pl.BlockSpec((1, w, tm), lambda b, i: (b, 0, i))
    vt_spec =pl.BlockSpec((1, MLA_HEADS, tm // TK, VT_ROWS, TK), lambda b, i: (b, 0, i, 0, 0))
    out_shape = (
        jax.ShapeDtypeStruct((B, MLA_HEADS * LANES, S), BF16),
        jax.ShapeDtypeStruct((B, S, MLA_HEADS * LANES), BF16),
        jax.ShapeDtypeStruct((B, MLA_HEADS, nt, VT_ROWS, TK), BF16),
        jax.ShapeDtypeStruct((B, DIFF_HEADS * LANES, S), BF16),
        jax.ShapeDtypeStruct((B, S, DIFF_HEADS * LANES), BF16),
        jax.ShapeDtypeStruct((B, DIFF_HEADS, nt, VT_ROWS, TK), BF16),
        jax.ShapeDtypeStruct((B, S, CH_WIDTH), BF16),
        jax.ShapeDtypeStruct((B, S, CH_WIDTH), BF16),
        jax.ShapeDtypeStruct((B, S, CH_WIDTH), BF16),
    )
    return pl.pallas_call(
        _inproj_kernel,
        grid=(B, S // tm),
        in_specs=[row(D), row(1), row(1), full(g), full(win), full(qn), full(wq), full(kvn),
                  full(wkv), full(inv), full(qcols)],
        out_specs=(col(MLA_HEADS * LANES), row(MLA_HEADS * LANES), vt_spec,
                   col(DIFF_HEADS * LANES), row(DIFF_HEADS * LANES), vt_spec,
                   row(CH_WIDTH), row(CH_WIDTH), row(CH_WIDTH)),
        out_shape=out_shape,
        compiler_params=pltpu.CompilerParams(
            dimension_semantics=("parallel", "parallel"), vmem_limit_bytes=VMEM_LIMIT),
        name="inproj",
    )(x, pos_col, rel_col, g, win, qn, wq, kvn, wkv, inv, qcols)


def _chunk_causal_tile_mask(n_q, tq, q_offset):
    r = lax.broadcasted_iota(jnp.int32, (TK, n_q), 0) // CHUNK
    c = (lax.broadcasted_iota(jnp.int32, (TK, n_q), 1) % tq + q_offset) // CHUNK
    return r <= c


def _store_scores(s, s_ref, smax_ref, slot, h, diag):
    if diag is not None:
        s = jnp.where(_chunk_causal_tile_mask(s.shape[1], *diag), s, -jnp.inf)
    s_ref[slot, h] = s
    smax_ref[slot, h] = jnp.max(s, axis=0, keepdims=True)


def _softmax_tile_update(vt, s_ref, smax_ref, m_ref, acc_ref, slot, h):
    m = m_ref[h]
    m_new = jnp.maximum(m, smax_ref[slot, h])
    p = jnp.exp2(s_ref[slot, h] - m_new).astype(BF16)
    acc_ref[h] = jnp.exp2(m - m_new) * acc_ref[h] + _dot(vt, p)
    m_ref[h] = m_new


def _softmax_state_init(m_ref, acc_ref):
    m_ref[...] = jnp.full(m_ref.shape, -jnp.inf, F32)
    acc_ref[...] = jnp.zeros(acc_ref.shape, F32)


def _softmax_result(acc_ref, h, dv):
    acc = acc_ref[h]
    return acc[0:dv] / acc[dv:dv + 1]


def _pipelined_key_sweep(n, heads, tile_context, score_head, consume_head):
    def step(score=None, consume=None):
        ctx = tile_context(score[0], score[2]) if score else None
        for h in range(heads):
            if score:
                score_head(*score, h, ctx)
            if consume:
                consume_head(*consume, h)

    @pl.when(n == 0)
    def _():
        step(score=(0, 0, True))
        step(consume=(0, 0))

    @pl.when(n > 0)
    def _():
        step(score=(0, 0, False))

        def body(jj, carry):
            j = 2 * jj
            step(score=(j + 1, 1, False), consume=(j, 0))
            step(score=(j + 2, 0, False), consume=(j + 1, 1))
            return carry

        lax.fori_loop(0, (n - 1) // 2, body, 0)

        @pl.when(n % 2 == 1)
        def _():
            step(score=(n, 1, True), consume=(n - 1, 0))
            step(consume=(n, 1))

        @pl.when(n % 2 == 0)
        def _():
            step(score=(n - 1, 1, False), consume=(n - 2, 0))
            step(score=(n, 0, True), consume=(n - 1, 1))
            step(consume=(n, 0))


def _softmax_scratch(heads, n):
    return [pltpu.VMEM((2, heads, TK, n), F32), pltpu.VMEM((2, heads, 1, n), F32),
            pltpu.VMEM((heads, 1, n), F32), pltpu.VMEM((heads, VT_ROWS, n), F32)]


def _mla_kernel(q_ref, k_ref, vt_ref, o_ref, s_ref, smax_ref, m_ref, acc_ref):
    i = pl.program_id(2)
    heads = m_ref.shape[0]
    _softmax_state_init(m_ref, acc_ref)

    def tile_context(j, diag):
        return pl.multiple_of(j * TK, TK)

    def score_head(j, slot, diag, h, start):
        sl = slice(h * LANES, (h + 1) * LANES)
        s = _dot(k_ref[0, pl.ds(start, TK), sl], q_ref[0, sl, :])
        _store_scores(s, s_ref, smax_ref, slot, h, (MLA_TQ, 0) if diag else None)

    def consume_head(j, slot, h):
        _softmax_tile_update(vt_ref[0, h, j], s_ref, smax_ref, m_ref, acc_ref, slot, h)

    assert MLA_TQ == TK
    _pipelined_key_sweep(i, heads, tile_context, score_head, consume_head)
    for h in range(heads):
        o_ref[0, h * MLA_V:(h + 1) * MLA_V, :] = _softmax_result(acc_ref, h, MLA_V)


def _mla_attention(qm, km, vmt):
    B, S, _ = km.shape
    nt = S // TK
    hb = MLA_HEADS_PER_STEP
    once = pl.Buffered(1)
    return pl.pallas_call(
        _mla_kernel,
        grid=(B, MLA_HEADS // hb, S // MLA_TQ),
        in_specs=[
            pl.BlockSpec((1, hb * LANES, MLA_TQ), lambda b, h, i: (b, h, i)),
            pl.BlockSpec((1, S, hb * LANES), lambda b, h, i: (b, 0, h), pipeline_mode=once),
            pl.BlockSpec((1, hb, nt, VT_ROWS, TK), lambda b, h, i: (b, h, 0, 0, 0),
                         pipeline_mode=once),
        ],
        out_specs=pl.BlockSpec((1, hb * MLA_V, MLA_TQ), lambda b, h, i: (b, h, i)),
        out_shape=jax.ShapeDtypeStruct((B, MLA_WIDTH, S), F32),
        scratch_shapes=_softmax_scratch(hb, MLA_TQ),
        compiler_params=pltpu.CompilerParams(
            dimension_semantics=("parallel", "parallel", "arbitrary"),
            vmem_limit_bytes=VMEM_LIMIT),
        name="mla_attn",
    )(qm, km, vmt)


def _diff_kernel(ordered_ref, q_ref, k_ref, vt_ref, posq_ref, posk_ref, slope_ref, lam_ref, g_ref,
                 o_ref, qq_ref, s_ref, smax_ref, m_ref, acc_ref, *, lam_init):
    b = pl.program_id(0)
    i = pl.program_id(2)
    heads = m_ref.shape[0]
    _softmax_state_init(m_ref, acc_ref)
    dim = lax.broadcasted_iota(jnp.int32, (LANES, DIFF_TQ), 0)
    pos_q = posq_ref[0]
    q_tiles_per_key_tile = TK // DIFF_TQ
    diag_tile = i // q_tiles_per_key_tile
    q_offset = (i % q_tiles_per_key_tile) * DIFF_TQ

    def stack_queries(with_pos_cols):
        for h in range(heads):
            q = q_ref[0, h * LANES:(h + 1) * LANES, :]
            for mp in range(2):
                keep = (dim >= mp * DIFF_QK) & (dim < (mp + 1) * DIFF_QK)
                if with_pos_cols:
                    keep = keep | (dim >= POS_COL0)
                qq_ref[h, :, mp * DIFF_TQ:(mp + 1) * DIFF_TQ] = jnp.where(keep, q, jnp.zeros_like(q))

    def make_scores(ordered):
        def tile_context(j, diag):
            start = pl.multiple_of(j * TK, TK)
            if not (diag or not ordered):
                return start, None
            d = posk_ref[0, pl.ds(start, TK), :] - pos_q
            return start, (2.0 * jnp.maximum(d, 0.0) if ordered else jnp.abs(d))

        def score_head(j, slot, diag, h, ctx):
            start, dist = ctx
            s = _dot(k_ref[0, pl.ds(start, TK), h * LANES:(h + 1) * LANES], qq_ref[h])
            if dist is not None:
                bias = dist * slope_ref[h]
                s = s - jnp.concatenate([bias, bias], axis=1)
            _store_scores(s, s_ref, smax_ref, slot, h, (DIFF_TQ, q_offset) if diag else None)
        return tile_context, score_head

    def consume_head(j, slot, h):
        _softmax_tile_update(vt_ref[0, h, j], s_ref, smax_ref, m_ref, acc_ref, slot, h)

    @pl.when(ordered_ref[b, i] != 0)
    def _():
        stack_queries(True)
        _pipelined_key_sweep(diag_tile, heads, *make_scores(True), consume_head)

    @pl.when(ordered_ref[b, i] == 0)
    def _():
        stack_queries(False)
        _pipelined_key_sweep(diag_tile, heads, *make_scores(False), consume_head)

    lam = lam_ref[...]
    lam_full = (jnp.exp(jnp.sum(lam[0:1] * lam[1:2], keepdims=True))
                - jnp.exp(jnp.sum(lam[2:3] * lam[3:4], keepdims=True)) + lam_init)
    for h in range(heads):
        o = _softmax_result(acc_ref, h, DIFF_V)
        od = o[:, :DIFF_TQ] - lam_full * o[:, DIFF_TQ:]
        o_ref[0, h * DIFF_V:(h + 1) * DIFF_V, :] = _rms(od, g_ref[h], axis=0) * (1.0 - lam_init)


def _diff_attention(ordered, qd, kd, vdt, rel_row, rel_col, slopes, lam, sub_norm, lam_init):
    B, S, _ = kd.shape
    nt = S // TK
    hb = DIFF_HEADS_PER_STEP
    once = pl.Buffered(1)
    return pl.pallas_call(
        functools.partial(_diff_kernel, lam_init=lam_init),
        grid=(B, DIFF_HEADS // hb, S // DIFF_TQ),
        in_specs=[
            pl.BlockSpec(memory_space=pltpu.SMEM),
            pl.BlockSpec((1, hb * LANES, DIFF_TQ), lambda b, h, i: (b, h, i)),
            pl.BlockSpec((1, S, hb * LANES), lambda b, h, i: (b, 0, h), pipeline_mode=once),
            pl.BlockSpec((1, hb, nt, VT_ROWS, TK), lambda b, h, i: (b, h, 0, 0, 0),
                         pipeline_mode=once),
            pl.BlockSpec((1, 1, DIFF_TQ), lambda b, h, i: (b, 0, i)),
            pl.BlockSpec((1, S, 1), lambda b, h, i: (b, 0, 0), pipeline_mode=once),
            pl.BlockSpec((hb, 1, 1), lambda b, h, i: (h, 0, 0)),
            pl.BlockSpec(lam.shape, lambda b, h, i: (0, 0)),
            pl.BlockSpec((hb, DIFF_V, 1), lambda b, h, i: (h, 0, 0)),
        ],
        out_specs=pl.BlockSpec((1, hb * DIFF_V, DIFF_TQ), lambda b, h, i: (b, h, i)),
        out_shape=jax.ShapeDtypeStruct((B, DIFF_WIDTH, S), F32),
        scratch_shapes=([pltpu.VMEM((hb, LANES, 2 * DIFF_TQ), BF16)]
                        + _softmax_scratch(hb, 2 * DIFF_TQ)),
        compiler_params=pltpu.CompilerParams(
            dimension_semantics=("parallel", "parallel", "arbitrary"),
            vmem_limit_bytes=VMEM_LIMIT),
        name="diff_attn",
    )(ordered, qd, kd, vdt, rel_row, rel_col, slopes, lam, sub_norm)


CH_PAD = CH_LEFT * CHUNK
CH_ROWS = 256


def _chunk_kernel(q_ref, k_ref, v_ref, diag_ref, o_ref, bias_ref):
    i = pl.program_id(1)
    heads_per_group = LANES // CH_HD
    groups = CH_WIDTH // LANES
    chunks = CH_ROWS // CHUNK

    @pl.when(i == 0)
    def _():
        for h in range(CH_HEADS):
            rows = jnp.broadcast_to(diag_ref[h:h + 1, :], (CHUNK, diag_ref.shape[1]))
            bias_ref[h] = pltpu.roll(rows, 0, 1, stride=1, stride_axis=0)[:, :CH_BAND]

    lane = lax.broadcasted_iota(jnp.int32, (CHUNK, LANES), 1)
    key_col = lax.broadcasted_iota(jnp.int32, (CHUNK, CH_BAND), 1)
    blocks = [(c, g) for c in range(chunks) for g in range(groups)]
    starts = [pl.multiple_of((i * chunks + c) * CHUNK, CHUNK) for c in range(chunks)]

    scores = []
    for c, g in blocks:
        gl = slice(g * LANES, (g + 1) * LANES)
        q = q_ref[0, c * CHUNK:(c + 1) * CHUNK, gl]
        qq = jnp.concatenate(
            [jnp.where((lane >= hh * CH_HD) & (lane < (hh + 1) * CH_HD), q, jnp.zeros_like(q))
             for hh in range(heads_per_group)], axis=0)
        scores.append(_dot_nt(qq, k_ref[0, pl.ds(starts[c], CH_BAND), gl]))

    probs, denoms = [], []
    for (c, g), s in zip(blocks, scores):
        valid = key_col >= CH_PAD - starts[c]
        s = jnp.concatenate(
            [jnp.where(valid, s[hh * CHUNK:(hh + 1) * CHUNK] + bias_ref[g * heads_per_group + hh],
                       -jnp.inf) for hh in range(heads_per_group)], axis=0)
        p = jnp.exp2(s - jnp.max(s, axis=1, keepdims=True))
        denoms.append(jnp.sum(p, axis=1, keepdims=True))
        probs.append(p.astype(BF16))

    for (c, g), p, l in zip(blocks, probs, denoms):
        gl = slice(g * LANES, (g + 1) * LANES)
        o = _dot(p, v_ref[0, pl.ds(starts[c], CH_BAND), gl]) / l
        o_ref[0, c * CHUNK:(c + 1) * CHUNK, gl] = jnp.where(lane < CH_HD, o[:CHUNK], o[CHUNK:])


def _chunk_attention(qc, kc_pad, vc_pad, diag):
    B, S, _ = qc.shape
    return pl.pallas_call(
        _chunk_kernel,
        grid=(B, S // CH_ROWS),
        in_specs=[
            pl.BlockSpec((1, CH_ROWS, CH_WIDTH), lambda b, i: (b, i, 0)),
            pl.BlockSpec((1, S + CH_PAD, CH_WIDTH), lambda b, i: (b, 0, 0)),
            pl.BlockSpec((1, S + CH_PAD, CH_WIDTH), lambda b, i: (b, 0, 0)),
            pl.BlockSpec(diag.shape, lambda b, i: (0, 0)),
        ],
        out_specs=pl.BlockSpec((1, CH_ROWS, CH_WIDTH), lambda b, i: (b, i, 0)),
        out_shape=jax.ShapeDtypeStruct((B, S, CH_WIDTH), F32),
        scratch_shapes=[pltpu.VMEM((CH_HEADS, CHUNK, CH_BAND), F32)],
        compiler_params=pltpu.CompilerParams(
            dimension_semantics=("parallel", "arbitrary"), vmem_limit_bytes=VMEM_LIMIT),
        name="chunk_attn",
    )(qc, kc_pad, vc_pad, diag)


def _mixer_outputs(omt_ref, odt_ref, oc_ref, gm_ref, gc_ref):
    om = _rms(omt_ref[0], gm_ref[...], axis=0).T.astype(BF16)
    od = odt_ref[0].T.astype(BF16)
    oc = _rms(oc_ref[0], gc_ref[...]).astype(BF16)
    return jnp.concatenate([om, od, oc], axis=1)


HALO = 8


def _ffn_kernel(omt_ref, odt_ref, oc_ref, x_ref, gm_ref, gc_ref, wp_ref, g_ref, wi_ref, cw_ref,
                cb_ref, wo_ref, gf_ref, o_ref, h_ref, carry_ref, *, final_norm):
    i = pl.program_id(1)
    tm = x_ref.shape[1]

    @pl.when(i == 0)
    def _():
        carry_ref[...] = jnp.zeros(carry_ref.shape, F32)

    x = x_ref[0] + _dot(_mixer_outputs(omt_ref, odt_ref, oc_ref, gm_ref, gc_ref), wp_ref[...])
    xn = _rms(x, g_ref[...]).astype(BF16)

    def project(c):
        lo, w = FF_CHUNKS[c]
        for half in range(2):
            col = half * D_FF + lo
            h_ref[c % 2, half, HALO:HALO + tm, 0:w] = _dot(xn, wi_ref[:, col:col + w])

    def gate(c):
        lo, w = FF_CHUNKS[c]
        halves = []
        for half in range(2):
            col = half * D_FF + lo
            h = h_ref.at[c % 2, half]
            h[0:HALO, 0:w] = carry_ref[half, :, lo:lo + w]
            carry_ref[half, :, lo:lo + w] = h[tm:tm + HALO, 0:w]
            cw = cw_ref[:, col:col + w]
            halves.append(cw[0:1] * h[HALO - 2:HALO - 2 + tm, 0:w]
                          + cw[1:2] * h[HALO - 1:HALO - 1 + tm, 0:w]
                          + cw[2:3] * h[HALO:HALO + tm, 0:w] + cb_ref[:, col:col + w])
        u, g = halves
        return (g / (1.0 + jnp.exp(-g)) * u).astype(BF16)

    project(0)
    y = x
    for c, (lo, w) in enumerate(FF_CHUNKS):
        if c + 1 < len(FF_CHUNKS):
            project(c + 1)
        y = y + _dot(gate(c), wo_ref[lo:lo + w, :])
    if final_norm:
        y = _rms(y, gf_ref[...])
    o_ref[0] = y


def _outproj_ffn(omt, odt, oc, x, gm, gc, w_proj, g, w_in, conv_w, conv_b, w_out, g_final,
                 final_norm):
    B, S, D = x.shape
    tm = min(ROW_TILE, S)
    resident = lambda a: pl.BlockSpec(a.shape, lambda b, i: (0,) * a.ndim,
                                      pipeline_mode=pl.Buffered(1))
    chunk_w = max(w for _, w in FF_CHUNKS)
    return pl.pallas_call(
        functools.partial(_ffn_kernel, final_norm=final_norm),
        grid=(B, S // tm),
        in_specs=[
            pl.BlockSpec((1, MLA_WIDTH, tm), lambda b, i: (b, 0, i)),
            pl.BlockSpec((1, DIFF_WIDTH, tm), lambda b, i: (b, 0, i)),
            pl.BlockSpec((1, tm, CH_WIDTH), lambda b, i: (b, i, 0)),
            pl.BlockSpec((1, tm, D), lambda b, i: (b, i, 0)),
            resident(gm), resident(gc), resident(w_proj),
            resident(g), resident(w_in), resident(conv_w), resident(conv_b), resident(w_out),
            resident(g_final),
        ],
        out_specs=pl.BlockSpec((1, tm, D), lambda b, i: (b, i, 0)),
        out_shape=jax.ShapeDtypeStruct((B, S, D), F32),
        scratch_shapes=[
            pltpu.VMEM((2, 2, HALO + tm, chunk_w), F32),
            pltpu.VMEM((2, HALO, D_FF), F32),
        ],
        compiler_params=pltpu.CompilerParams(
            dimension_semantics=("parallel", "arbitrary"), vmem_limit_bytes=VMEM_LIMIT),
        name="outproj_ffn",
    )(omt, odt, oc, x, gm, gc, w_proj, g, w_in, conv_w, conv_b, w_out, g_final)


def _rotate_half_cols(w):
    half = w.shape[1] // 2
    return jnp.concatenate([-w[:, half:], w[:, :half]], axis=1)


def _place(cols, offset, width=LANES):
    return jnp.pad(cols, ((0, 0), (offset, width - offset - cols.shape[1])))


def _layout_w_in(w):
    kr = w[:, MLA_Q_RANK + MLA_KV_RANK:IN_MLA]
    return jnp.concatenate([
        w[:, :MLA_Q_RANK + MLA_KV_RANK],
        _place(kr, MLA_NOPE), _place(_rotate_half_cols(kr), MLA_NOPE),
        w[:, IN_MLA:],
    ], axis=1).astype(BF16)


def _layout_w_uq(w):
    hd = MLA_NOPE + MLA_ROPE
    plain, rot = [], []
    for h in range(MLA_HEADS):
        wh = w[:, h * hd:(h + 1) * hd]
        plain.append(_place(wh, 0))
        rot.append(_place(_rotate_half_cols(wh[:, MLA_NOPE:]), MLA_NOPE))
    return jnp.concatenate(plain + rot, axis=1).astype(BF16)


def _layout_w_ukv(w):
    hd = MLA_NOPE + MLA_V
    ks = [_place(w[:, h * hd:h * hd + MLA_NOPE], 0) for h in range(MLA_HEADS)]
    vs = [w[:, h * hd + MLA_NOPE:(h + 1) * hd] for h in range(MLA_HEADS)]
    return jnp.concatenate(ks + vs, axis=1).astype(BF16)


def _slope_columns(slopes2):
    p1 = slopes2.astype(BF16).astype(F32)
    p2 = (slopes2 - p1).astype(BF16).astype(F32)
    p3 = (slopes2 - p1 - p2).astype(BF16).astype(F32)
    pieces = jnp.stack([p1, p2, p3], axis=1)
    cols = jnp.concatenate([pieces * 65536.0, pieces * 256.0, pieces], axis=1)
    return _place(cols, POS_COL0).reshape(1, -1)


def _ordered_query_tiles(rel):
    B, S = rel.shape
    query_min = jnp.min(rel.reshape(B, S // DIFF_TQ, DIFF_TQ), axis=2)
    key_max = lax.cummax(jnp.max(rel.reshape(B, S // TK, TK), axis=2), axis=1)
    lowest = jnp.full((B, 1), jnp.iinfo(rel.dtype).min, rel.dtype)
    before = jnp.concatenate([lowest, key_max[:, :-1]], axis=1)
    before = jnp.repeat(before, TK // DIFF_TQ, axis=1)
    return (before <= query_min).astype(jnp.int32)


def _chunk_bias_diagonals(rel_bias):
    n_rel = rel_bias.shape[1]
    far = jnp.broadcast_to(rel_bias[:, n_rel - 1:], (rel_bias.shape[0], REL_MAX))
    wrap = far[:, :CHUNK]
    return jnp.concatenate([far, rel_bias[:, ::-1], wrap], axis=1) * LOG2E


def kernel(x, positions, attn_norm, w_in, mla_q_norm, mla_w_uq, mla_kv_norm, mla_w_ukv, diff_lambda, diff_norm, chunk_rel_bias, mla_out_norm, chunk_out_norm, w_out, ffn_norm, w_ffn_in, ffn_conv_w, ffn_conv_b, w_ffn_out, final_norm):
    B, S, D = x.shape
    pos_col = positions.astype(F32)[:, :, None]
    rel = positions - positions[:, :1]
    rel_f = rel.astype(F32)
    half = MLA_ROPE // 2
    inv = ROPE_THETA ** (-jnp.arange(half, dtype=F32) / half)
    inv_slot = _place(jnp.concatenate([inv, inv])[None, :], MLA_NOPE)
    slopes = 2.0 ** (-8.0 * jnp.arange(1, DIFF_HEADS + 1, dtype=F32) / DIFF_HEADS)
    slopes2 = slopes * LOG2E
    ordered = _ordered_query_tiles(rel)
    slope_cols = _slope_columns(slopes2)
    slopes2 = slopes2.reshape(DIFF_HEADS, 1, 1)
    row = lambda v: v.reshape(1, -1)

    for l in range(DEPTH):
        lam_init = 0.8 - 0.6 * math.exp(-0.3 * l)
        qm, km, vmt, qd, kd, vdt, qc, kc, vc = _inproj(
            x, pos_col, rel[:, :, None], row(attn_norm[l]), _layout_w_in(w_in[l]),
            row(mla_q_norm[l]), _layout_w_uq(mla_w_uq[l]), row(mla_kv_norm[l]),
            _layout_w_ukv(mla_w_ukv[l]), inv_slot, slope_cols)
        omt = _mla_attention(qm, km, vmt)
        odt = _diff_attention(ordered, qd, kd, vdt, rel_f[:, None, :], rel_f[:, :, None], slopes2,
                              diff_lambda[l], diff_norm[l].reshape(DIFF_HEADS, DIFF_V, 1), lam_init)
        pad = ((0, 0), (CH_PAD, 0), (0, 0))
        oc = _chunk_attention(qc, jnp.pad(kc, pad), jnp.pad(vc, pad),
                              _chunk_bias_diagonals(chunk_rel_bias[l]))
        x = _outproj_ffn(omt, odt, oc, x, mla_out_norm[l].reshape(-1, 1), row(chunk_out_norm[l]),
                         w_out[l].astype(BF16), row(ffn_norm[l]), w_ffn_in[l].astype(BF16),
                         ffn_conv_w[l], row(ffn_conv_b[l]), w_ffn_out[l].astype(BF16),
                         row(final_norm), final_norm=(l == DEPTH - 1))
    return x
```

```python
import functools
import math

import jax
import jax.numpy as jnp
from jax import lax
from jax.experimental import pallas as pl
from jax.experimental.pallas import tpu as pltpu

F32 = jnp.float32
BF16 = jnp.bfloat16

D_MODEL = 1024
DEPTH = 2
CHUNK = 64
EPS = 1e-6

MLA_HEADS = 6
MLA_Q_RANK = 256
MLA_KV_RANK = 128
MLA_NOPE = 64
MLA_ROPE = 32
MLA_V = 64
ROPE_THETA = 10000.0

DIFF_HEADS = 6
DIFF_QK = 32
DIFF_V = 64

CH_HEADS = 4
CH_HD = 64
CH_LEFT = 8
CH_BAND = (CH_LEFT + 1) * CHUNK
REL_MAX = 256

MLA_WIDTH = MLA_HEADS * MLA_V
DIFF_WIDTH = DIFF_HEADS * DIFF_V
CH_WIDTH = CH_HEADS * CH_HD
DIFF_QK_WIDTH = DIFF_HEADS * 2 * DIFF_QK

IN_MLA = MLA_Q_RANK + MLA_KV_RANK + MLA_ROPE
D_FF = 2816

LOG2E = math.log2(math.e)

LANES = 128

ROW_TILE = 512
TK = 512
MLA_TQ = 512
DIFF_TQ = 256
FF_CHUNKS = ((0, 1024), (1024, 1024), (2048, 768))
VT_ROWS = 80
MLA_HEADS_PER_STEP = 6
DIFF_HEADS_PER_STEP = 6
POS_COL0 = 2 * DIFF_QK
VMEM_LIMIT = 56 * 1024 * 1024

P_CQ = 0
P_CKV = MLA_Q_RANK
P_KR = P_CKV + MLA_KV_RANK
P_KRR = P_KR + LANES
P_DQ = P_KRR + LANES
P_DK = P_DQ + DIFF_QK_WIDTH
P_DV = P_DK + DIFF_QK_WIDTH
P_CQ_ = P_DV + DIFF_WIDTH
P_CK_ = P_CQ_ + CH_WIDTH
P_CV_ = P_CK_ + CH_WIDTH
P_TOTAL = P_CV_ + CH_WIDTH


def _rms(x, g, axis=-1):
    return x * lax.rsqrt(jnp.mean(x * x, axis=axis, keepdims=True) + EPS) * g


def _dot(a, b):
    return jnp.dot(a, b, preferred_element_type=F32)


def _dot_nt(a, b):
    return lax.dot_general(a, b, (((1,), (1,)), ((), ())), preferred_element_type=F32)


def _inproj_kernel(x_ref, pos_ref, rel_ref, g_ref, win_ref, qn_ref, wq_ref, kvn_ref, wkv_ref,
                   inv_ref, qcols_ref, qm_ref, km_ref, vmt_ref, qd_ref, kd_ref, vdt_ref, qc_ref, kc_ref, vc_ref):
    tm = x_ref.shape[1]
    xn = _rms(x_ref[0], g_ref[...]).astype(BF16)
    proj = _dot(xn, win_ref[...])

    ang = pos_ref[0] * inv_ref[...]
    cos = jnp.cos(ang)
    sin = jnp.sin(ang)

    cqn = _rms(proj[:, P_CQ:P_CKV], qn_ref[...]).astype(BF16)
    q2 = _dot(cqn, wq_ref[...])
    ckvn = _rms(proj[:, P_CKV:P_KR], kvn_ref[...]).astype(BF16)
    kv2 = _dot(ckvn, wkv_ref[...])

    k_rope = proj[:, P_KR:P_KRR] * cos + proj[:, P_KRR:P_DQ] * sin
    q_scale = (MLA_NOPE + MLA_ROPE) ** -0.5 * LOG2E
    nh = MLA_HEADS * LANES
    q_heads = []
    for h in range(MLA_HEADS):
        sl = slice(h * LANES, (h + 1) * LANES)
        qh = q2[:, sl] * cos + q2[:, nh + h * LANES: nh + (h + 1) * LANES] * sin
        q_heads.append(qh * q_scale)
        km_ref[0, :, sl] = (kv2[:, sl] + k_rope).astype(BF16)
    qm_ref[0] = jnp.concatenate(q_heads, axis=1).T.astype(BF16)

    vm_t = kv2[:, nh:nh + MLA_WIDTH].T.astype(BF16)
    vd_t = proj[:, P_DV:P_CQ_].T.astype(BF16)
    ones_rows = jnp.where(lax.broadcasted_iota(jnp.int32, (VT_ROWS - MLA_V, TK), 0) == 0,
                          1.0, 0.0).astype(BF16)
    for h in range(MLA_HEADS):
        for t in range(tm // TK):
            vmt_ref[0, h, t, 0:MLA_V, :] = vm_t[h * MLA_V:(h + 1) * MLA_V, t * TK:(t + 1) * TK]
            vdt_ref[0, h, t, 0:DIFF_V, :] = vd_t[h * DIFF_V:(h + 1) * DIFF_V, t * TK:(t + 1) * TK]
            vmt_ref[0, h, t, MLA_V:VT_ROWS, :] = ones_rows
            vdt_ref[0, h, t, DIFF_V:VT_ROWS, :] = ones_rows

    lane = lax.broadcasted_iota(jnp.int32, (1, LANES), 1)
    rel = rel_ref[0]
    digits = [(rel >> 16).astype(F32), ((rel >> 8) & 255).astype(F32), (rel & 255).astype(F32)]
    in_cols = (lane >= POS_COL0) & (lane < POS_COL0 + 9)
    k_cols = jnp.where(in_cols, jnp.where(lane < POS_COL0 + 3, digits[0],
                                          jnp.where(lane < POS_COL0 + 6, digits[1], digits[2])), 0.0)
    qd = proj[:, P_DQ:P_DK] * (DIFF_QK ** -0.5 * LOG2E)
    kd = proj[:, P_DK:P_DV]
    heads_per_group = LANES // (2 * DIFF_QK)
    q_heads = []
    for h in range(DIFF_HEADS):
        gl = slice((h // heads_per_group) * LANES, (h // heads_per_group + 1) * LANES)
        sl = slice(h * LANES, (h + 1) * LANES)
        qh, kh = qd[:, gl], kd[:, gl]
        if h % heads_per_group:
            qh, kh = pltpu.roll(qh, 2 * DIFF_QK, 1), pltpu.roll(kh, 2 * DIFF_QK, 1)
        q_heads.append(jnp.where(lane < 2 * DIFF_QK, qh, qcols_ref[:, sl]))
        kd_ref[0, :, sl] = jnp.where(lane < 2 * DIFF_QK, kh, k_cols).astype(BF16)
    qd_ref[0] = jnp.concatenate(q_heads, axis=1).T.astype(BF16)
    qc_ref[0] = (proj[:, P_CQ_:P_CK_] * (CH_HD ** -0.5 * LOG2E)).astype(BF16)
    kc_ref[0] = proj[:, P_CK_:P_CV_].astype(BF16)
    vc_ref[0] = proj[:, P_CV_:P_TOTAL].astype(BF16)


def _inproj(x, pos_col, rel_col, g, win, qn, wq, kvn, wkv, inv, qcols):
    B, S, D = x.shape
    tm = min(ROW_TILE, S)
    nt = S // TK
    full = lambda a: pl.BlockSpec(a.shape, lambda b, i: (0,) * a.ndim)
    row = lambda w: pl.BlockSpec((1, tm, w), lambda b, i: (b, i, 0))
    col = lambda w: pl.BlockSpec((1, w, tm), lambda b, i: (b, 0, i))
    vt_spec =pl.BlockSpec((1, MLA_HEADS, tm // TK, VT_ROWS, TK), lambda b, i: (b, 0, i, 0, 0))
    out_shape = (
        jax.ShapeDtypeStruct((B, MLA_HEADS * LANES, S), BF16),
        jax.ShapeDtypeStruct((B, S, MLA_HEADS * LANES), BF16),
        jax.ShapeDtypeStruct((B, MLA_HEADS, nt, VT_ROWS, TK), BF16),
        jax.ShapeDtypeStruct((B, DIFF_HEADS * LANES, S), BF16),
        jax.ShapeDtypeStruct((B, S, DIFF_HEADS * LANES), BF16),
        jax.ShapeDtypeStruct((B, DIFF_HEADS, nt, VT_ROWS, TK), BF16),
        jax.ShapeDtypeStruct((B, S, CH_WIDTH), BF16),
        jax.ShapeDtypeStruct((B, S, CH_WIDTH), BF16),
        jax.ShapeDtypeStruct((B, S, CH_WIDTH), BF16),
    )
    return pl.pallas_call(
        _inproj_kernel,
        grid=(B, S // tm),
        in_specs=[row(D), row(1), row(1), full(g), full(win), full(qn), full(wq), full(kvn),
                  full(wkv), full(inv), full(qcols)],
        out_specs=(col(MLA_HEADS * LANES), row(MLA_HEADS * LANES), vt_spec,
                   col(DIFF_HEADS * LANES), row(DIFF_HEADS * LANES), vt_spec,
                   row(CH_WIDTH), row(CH_WIDTH), row(CH_WIDTH)),
        out_shape=out_shape,
        compiler_params=pltpu.CompilerParams(
            dimension_semantics=("parallel", "parallel"), vmem_limit_bytes=VMEM_LIMIT),
        name="inproj",
    )(x, pos_col, rel_col, g, win, qn, wq, kvn, wkv, inv, qcols)


def _chunk_causal_tile_mask(n_q, tq, q_offset):
    r = lax.broadcasted_iota(jnp.int32, (TK, n_q), 0) // CHUNK
    c = (lax.broadcasted_iota(jnp.int32, (TK, n_q), 1) % tq + q_offset) // CHUNK
    return r <= c


def _store_scores(s, s_ref, smax_ref, slot, h, diag):
    if diag is not None:
        s = jnp.where(_chunk_causal_tile_mask(s.shape[1], *diag), s, -jnp.inf)
    s_ref[slot, h, :, 0:s.shape[1]] = s
    smax_ref[slot, h] = jnp.max(s, axis=0, keepdims=True)


def _softmax_tile_update(vt, s_ref, smax_ref, m_ref, acc_ref, slot, h):
    m = m_ref[h]
    m_new = jnp.maximum(m, smax_ref[slot, h])
    p = jnp.exp2(s_ref[slot, h, :, 0:m.shape[1]] - m_new).astype(BF16)
    acc_ref[h] = jnp.exp2(m - m_new) * acc_ref[h] + _dot(vt, p)
    m_ref[h] = m_new


def _softmax_state_init(m_ref, acc_ref):
    m_ref[...] = jnp.full(m_ref.shape, -jnp.inf, F32)
    acc_ref[...] = jnp.zeros(acc_ref.shape, F32)


def _softmax_result(acc_ref, h, dv):
    acc = acc_ref[h]
    return acc[0:dv] / acc[dv:dv + 1]


def _pipelined_key_sweep(n, heads, tile_context, score_head, consume_head):
    def step(score=None, consume=None):
        ctx = tile_context(score[0], score[2]) if score else None
        for h in range(heads):
            if score:
                score_head(*score, h, ctx)
            if consume:
                consume_head(*consume, h)

    @pl.when(n == 0)
    def _():
        step(score=(0, 0, True))
        step(consume=(0, 0))

    @pl.when(n > 0)
    def _():
        step(score=(0, 0, False))

        def body(jj, carry):
            j = 2 * jj
            step(score=(j + 1, 1, False), consume=(j, 0))
            step(score=(j + 2, 0, False), consume=(j + 1, 1))
            return carry

        lax.fori_loop(0, (n - 1) // 2, body, 0)

        @pl.when(n % 2 == 1)
        def _():
            step(score=(n, 1, True), consume=(n - 1, 0))
            step(consume=(n, 1))

        @pl.when(n % 2 == 0)
        def _():
            step(score=(n - 1, 1, False), consume=(n - 2, 0))
            step(score=(n, 0, True), consume=(n - 1, 1))
            step(consume=(n, 0))


def _softmax_scratch(heads, n):
    return [pltpu.VMEM((2, heads, TK, n + LANES), F32), pltpu.VMEM((2, heads, 1, n), F32),
            pltpu.VMEM((heads, 1, n), F32), pltpu.VMEM((heads, VT_ROWS, n), F32)]


def _mla_kernel(q_ref, k_ref, vt_ref, o_ref, s_ref, smax_ref, m_ref, acc_ref):
    i = pl.program_id(2)
    heads = m_ref.shape[0]
    _softmax_state_init(m_ref, acc_ref)

    def tile_context(j, diag):
        return pl.multiple_of(j * TK, TK)

    def score_head(j, slot, diag, h, start):
        sl = slice(h * LANES, (h + 1) * LANES)
        s = _dot(k_ref[0, pl.ds(start, TK), sl], q_ref[0, sl, :])
        _store_scores(s, s_ref, smax_ref, slot, h, (MLA_TQ, 0) if diag else None)

    def consume_head(j, slot, h):
        _softmax_tile_update(vt_ref[0, h, j], s_ref, smax_ref, m_ref, acc_ref, slot, h)

    assert MLA_TQ == TK
    _pipelined_key_sweep(i, heads, tile_context, score_head, consume_head)
    for h in range(heads):
        o_ref[0, h * MLA_V:(h + 1) * MLA_V, :] = _softmax_result(acc_ref, h, MLA_V)


def _mla_attention(qm, km, vmt):
    B, S, _ = km.shape
    nt = S // TK
    hb = MLA_HEADS_PER_STEP
    once = pl.Buffered(1)
    return pl.pallas_call(
        _mla_kernel,
        grid=(B, MLA_HEADS // hb, S // MLA_TQ),
        in_specs=[
            pl.BlockSpec((1, hb * LANES, MLA_TQ), lambda b, h, i: (b, h, i)),
            pl.BlockSpec((1, S, hb * LANES), lambda b, h, i: (b, 0, h), pipeline_mode=once),
            pl.BlockSpec((1, hb, nt, VT_ROWS, TK), lambda b, h, i: (b, h, 0, 0, 0),
                         pipeline_mode=once),
        ],
        out_specs=pl.BlockSpec((1, hb * MLA_V, MLA_TQ), lambda b, h, i: (b, h, i)),
        out_shape=jax.ShapeDtypeStruct((B, MLA_WIDTH, S), F32),
        scratch_shapes=_softmax_scratch(hb, MLA_TQ),
        compiler_params=pltpu.CompilerParams(
            dimension_semantics=("parallel", "parallel", "arbitrary"),
            vmem_limit_bytes=VMEM_LIMIT),
        name="mla_attn",
    )(qm, km, vmt)


def _diff_kernel(ordered_ref, q_ref, k_ref, vt_ref, posq_ref, posk_ref, slope_ref, lam_ref, g_ref,
                 o_ref, qq_ref, s_ref, smax_ref, m_ref, acc_ref, *, lam_init):
    b = pl.program_id(0)
    i = pl.program_id(2)
    heads = m_ref.shape[0]
    _softmax_state_init(m_ref, acc_ref)
    dim = lax.broadcasted_iota(jnp.int32, (LANES, DIFF_TQ), 0)
    pos_q = posq_ref[0]
    q_tiles_per_key_tile = TK // DIFF_TQ
    diag_tile = i // q_tiles_per_key_tile
    q_offset = (i % q_tiles_per_key_tile) * DIFF_TQ

    def stack_queries(with_pos_cols):
        for h in range(heads):
            q = q_ref[0, h * LANES:(h + 1) * LANES, :]
            for mp in range(2):
                keep = (dim >= mp * DIFF_QK) & (dim < (mp + 1) * DIFF_QK)
                if with_pos_cols:
                    keep = keep | (dim >= POS_COL0)
                qq_ref[h, :, mp * DIFF_TQ:(mp + 1) * DIFF_TQ] = jnp.where(keep, q, jnp.zeros_like(q))

    def make_scores(ordered):
        def tile_context(j, diag):
            start = pl.multiple_of(j * TK, TK)
            if not (diag or not ordered):
                return start, None
            d = posk_ref[0, pl.ds(start, TK), :] - pos_q
            return start, (2.0 * jnp.maximum(d, 0.0) if ordered else jnp.abs(d))

        def score_head(j, slot, diag, h, ctx):
            start, dist = ctx
            s = _dot(k_ref[0, pl.ds(start, TK), h * LANES:(h + 1) * LANES], qq_ref[h])
            if dist is not None:
                bias = dist * slope_ref[h]
                s = s - jnp.concatenate([bias, bias], axis=1)
            _store_scores(s, s_ref, smax_ref, slot, h, (DIFF_TQ, q_offset) if diag else None)
        return tile_context, score_head

    def consume_head(j, slot, h):
        _softmax_tile_update(vt_ref[0, h, j], s_ref, smax_ref, m_ref, acc_ref, slot, h)

    @pl.when(ordered_ref[b, i] != 0)
    def _():
        stack_queries(True)
        _pipelined_key_sweep(diag_tile, heads, *make_scores(True), consume_head)

    @pl.when(ordered_ref[b, i] == 0)
    def _():
        stack_queries(False)
        _pipelined_key_sweep(diag_tile, heads, *make_scores(False), consume_head)

    lam = lam_ref[...]
    lam_full = (jnp.exp(jnp.sum(lam[0:1] * lam[1:2], keepdims=True))
                - jnp.exp(jnp.sum(lam[2:3] * lam[3:4], keepdims=True)) + lam_init)
    for h in range(heads):
        o = _softmax_result(acc_ref, h, DIFF_V)
        od = o[:, :DIFF_TQ] - lam_full * o[:, DIFF_TQ:]
        o_ref[0, h * DIFF_V:(h + 1) * DIFF_V, :] = _rms(od, g_ref[h], axis=0) * (1.0 - lam_init)


def _diff_attention(ordered, qd, kd, vdt, rel_row, rel_col, slopes, lam, sub_norm, lam_init):
    B, S, _ = kd.shape
    nt = S // TK
    hb = DIFF_HEADS_PER_STEP
    once = pl.Buffered(1)
    return pl.pallas_call(
        functools.partial(_diff_kernel, lam_init=lam_init),
        grid=(B, DIFF_HEADS // hb, S // DIFF_TQ),
        in_specs=[
            pl.BlockSpec(memory_space=pltpu.SMEM),
            pl.BlockSpec((1, hb * LANES, DIFF_TQ), lambda b, h, i: (b, h, i)),
            pl.BlockSpec((1, S, hb * LANES), lambda b, h, i: (b, 0, h), pipeline_mode=once),
            pl.BlockSpec((1, hb, nt, VT_ROWS, TK), lambda b, h, i: (b, h, 0, 0, 0),
                         pipeline_mode=once),
            pl.BlockSpec((1, 1, DIFF_TQ), lambda b, h, i: (b, 0, i)),
            pl.BlockSpec((1, S, 1), lambda b, h, i: (b, 0, 0), pipeline_mode=once),
            pl.BlockSpec((hb, 1, 1), lambda b, h, i: (h, 0, 0)),
            pl.BlockSpec(lam.shape, lambda b, h, i: (0, 0)),
            pl.BlockSpec((hb, DIFF_V, 1), lambda b, h, i: (h, 0, 0)),
        ],
        out_specs=pl.BlockSpec((1, hb * DIFF_V, DIFF_TQ), lambda b, h, i: (b, h, i)),
        out_shape=jax.ShapeDtypeStruct((B, DIFF_WIDTH, S), F32),
        scratch_shapes=([pltpu.VMEM((hb, LANES, 2 * DIFF_TQ), BF16)]
                        + _softmax_scratch(hb, 2 * DIFF_TQ)),
        compiler_params=pltpu.CompilerParams(
            dimension_semantics=("parallel", "parallel", "arbitrary"),
            vmem_limit_bytes=VMEM_LIMIT),
        name="diff_attn",
    )(ordered, qd, kd, vdt, rel_row, rel_col, slopes, lam, sub_norm)


CH_PAD = CH_LEFT * CHUNK
CH_ROWS = 256


def _chunk_kernel(q_ref, k_ref, v_ref, diag_ref, o_ref, bias_ref):
    i = pl.program_id(1)
    heads_per_group = LANES // CH_HD
    groups = CH_WIDTH // LANES
    chunks = CH_ROWS // CHUNK

    @pl.when(i == 0)
    def _():
        for h in range(CH_HEADS):
            rows = jnp.broadcast_to(diag_ref[h:h + 1, :], (CHUNK, diag_ref.shape[1]))
            bias_ref[h] = pltpu.roll(rows, 0, 1, stride=1, stride_axis=0)[:, :CH_BAND]

    lane = lax.broadcasted_iota(jnp.int32, (CHUNK, LANES), 1)
    key_col = lax.broadcasted_iota(jnp.int32, (CHUNK, CH_BAND), 1)
    blocks = [(c, g) for c in range(chunks) for g in range(groups)]
    starts = [pl.multiple_of((i * chunks + c) * CHUNK, CHUNK) for c in range(chunks)]

    scores = []
    for c, g in blocks:
        gl = slice(g * LANES, (g + 1) * LANES)
        q = q_ref[0, c * CHUNK:(c + 1) * CHUNK, gl]
        qq = jnp.concatenate(
            [jnp.where((lane >= hh * CH_HD) & (lane < (hh + 1) * CH_HD), q, jnp.zeros_like(q))
             for hh in range(heads_per_group)], axis=0)
        scores.append(_dot_nt(qq, k_ref[0, pl.ds(starts[c], CH_BAND), gl]))

    probs, denoms = [], []
    for (c, g), s in zip(blocks, scores):
        valid = key_col >= CH_PAD - starts[c]
        s = jnp.concatenate(
            [jnp.where(valid, s[hh * CHUNK:(hh + 1) * CHUNK] + bias_ref[g * heads_per_group + hh],
                       -jnp.inf) for hh in range(heads_per_group)], axis=0)
        p = jnp.exp2(s - jnp.max(s, axis=1, keepdims=True))
        denoms.append(jnp.sum(p, axis=1, keepdims=True))
        probs.append(p.astype(BF16))

    for (c, g), p, l in zip(blocks, probs, denoms):
        gl = slice(g * LANES, (g + 1) * LANES)
        o = _dot(p, v_ref[0, pl.ds(starts[c], CH_BAND), gl]) / l
        o_ref[0, c * CHUNK:(c + 1) * CHUNK, gl] = jnp.where(lane < CH_HD, o[:CHUNK], o[CHUNK:])


def _chunk_attention(qc, kc_pad, vc_pad, diag):
    B, S, _ = qc.shape
    return pl.pallas_call(
        _chunk_kernel,
        grid=(B, S // CH_ROWS),
        in_specs=[
            pl.BlockSpec((1, CH_ROWS, CH_WIDTH), lambda b, i: (b, i, 0)),
            pl.BlockSpec((1, S + CH_PAD, CH_WIDTH), lambda b, i: (b, 0, 0)),
            pl.BlockSpec((1, S + CH_PAD, CH_WIDTH), lambda b, i: (b, 0, 0)),
            pl.BlockSpec(diag.shape, lambda b, i: (0, 0)),
        ],
        out_specs=pl.BlockSpec((1, CH_ROWS, CH_WIDTH), lambda b, i: (b, i, 0)),
        out_shape=jax.ShapeDtypeStruct((B, S, CH_WIDTH), F32),
        scratch_shapes=[pltpu.VMEM((CH_HEADS, CHUNK, CH_BAND), F32)],
        compiler_params=pltpu.CompilerParams(
            dimension_semantics=("parallel", "arbitrary"), vmem_limit_bytes=VMEM_LIMIT),
        name="chunk_attn",
    )(qc, kc_pad, vc_pad, diag)


def _mixer_outputs(omt_ref, odt_ref, oc_ref, gm_ref, gc_ref):
    om = _rms(omt_ref[0], gm_ref[...], axis=0).T.astype(BF16)
    od = odt_ref[0].T.astype(BF16)
    oc = _rms(oc_ref[0], gc_ref[...]).astype(BF16)
    return jnp.concatenate([om, od, oc], axis=1)


HALO = 8


def _ffn_kernel(omt_ref, odt_ref, oc_ref, x_ref, gm_ref, gc_ref, wp_ref, g_ref, wi_ref, cw_ref,
                cb_ref, wo_ref, gf_ref, o_ref, h_ref, carry_ref, *, final_norm):
    i = pl.program_id(1)
    tm = x_ref.shape[1]

    @pl.when(i == 0)
    def _():
        carry_ref[...] = jnp.zeros(carry_ref.shape, F32)

    x = x_ref[0] + _dot(_mixer_outputs(omt_ref, odt_ref, oc_ref, gm_ref, gc_ref), wp_ref[...])
    xn = _rms(x, g_ref[...]).astype(BF16)

    def project(c):
        lo, w = FF_CHUNKS[c]
        for half in range(2):
            col = half * D_FF + lo
            h_ref[c % 2, half, HALO:HALO + tm, 0:w] = _dot(xn, wi_ref[:, col:col + w])

    def gate(c):
        lo, w = FF_CHUNKS[c]
        halves = []
        for half in range(2):
            col = half * D_FF + lo
            h = h_ref.at[c % 2, half]
            h[0:HALO, 0:w] = carry_ref[half, :, lo:lo + w]
            carry_ref[half, :, lo:lo + w] = h[tm:tm + HALO, 0:w]
            cw = cw_ref[:, col:col + w]
            halves.append(cw[0:1] * h[HALO - 2:HALO - 2 + tm, 0:w]
                          + cw[1:2] * h[HALO - 1:HALO - 1 + tm, 0:w]
                          + cw[2:3] * h[HALO:HALO + tm, 0:w] + cb_ref[:, col:col + w])
        u, g = halves
        return (g / (1.0 + jnp.exp(-g)) * u).astype(BF16)

    project(0)
    y = x
    for c, (lo, w) in enumerate(FF_CHUNKS):
        if c + 1 < len(FF_CHUNKS):
            project(c + 1)
        y = y + _dot(gate(c), wo_ref[lo:lo + w, :])
    if final_norm:
        y = _rms(y, gf_ref[...])
    o_ref[0] = y


def _outproj_ffn(omt, odt, oc, x, gm, gc, w_proj, g, w_in, conv_w, conv_b, w_out, g_final,
                 final_norm):
    B, S, D = x.shape
    tm = min(ROW_TILE, S)
    resident = lambda a: pl.BlockSpec(a.shape, lambda b, i: (0,) * a.ndim,
                                      pipeline_mode=pl.Buffered(1))
    chunk_w = max(w for _, w in FF_CHUNKS)
    return pl.pallas_call(
        functools.partial(_ffn_kernel, final_norm=final_norm),
        grid=(B, S // tm),
        in_specs=[
            pl.BlockSpec((1, MLA_WIDTH, tm), lambda b, i: (b, 0, i)),
            pl.BlockSpec((1, DIFF_WIDTH, tm), lambda b, i: (b, 0, i)),
            pl.BlockSpec((1, tm, CH_WIDTH), lambda b, i: (b, i, 0)),
            pl.BlockSpec((1, tm, D), lambda b, i: (b, i, 0)),
            resident(gm), resident(gc), resident(w_proj),
            resident(g), resident(w_in), resident(conv_w), resident(conv_b), resident(w_out),
            resident(g_final),
        ],
        out_specs=pl.BlockSpec((1, tm, D), lambda b, i: (b, i, 0)),
        out_shape=jax.ShapeDtypeStruct((B, S, D), F32),
        scratch_shapes=[
            pltpu.VMEM((2, 2, HALO + tm, chunk_w), F32),
            pltpu.VMEM((2, HALO, D_FF), F32),
        ],
        compiler_params=pltpu.CompilerParams(
            dimension_semantics=("parallel", "arbitrary"), vmem_limit_bytes=VMEM_LIMIT),
        name="outproj_ffn",
    )(omt, odt, oc, x, gm, gc, w_proj, g, w_in, conv_w, conv_b, w_out, g_final)


def _rotate_half_cols(w):
    half = w.shape[1] // 2
    return jnp.concatenate([-w[:, half:], w[:, :half]], axis=1)


def _place(cols, offset, width=LANES):
    return jnp.pad(cols, ((0, 0), (offset, width - offset - cols.shape[1])))


def _layout_w_in(w):
    kr = w[:, MLA_Q_RANK + MLA_KV_RANK:IN_MLA]
    return jnp.concatenate([
        w[:, :MLA_Q_RANK + MLA_KV_RANK],
        _place(kr, MLA_NOPE), _place(_rotate_half_cols(kr), MLA_NOPE),
        w[:, IN_MLA:],
    ], axis=1).astype(BF16)


def _layout_w_uq(w):
    hd = MLA_NOPE + MLA_ROPE
    plain, rot = [], []
    for h in range(MLA_HEADS):
        wh = w[:, h * hd:(h + 1) * hd]
        plain.append(_place(wh, 0))
        rot.append(_place(_rotate_half_cols(wh[:, MLA_NOPE:]), MLA_NOPE))
    return jnp.concatenate(plain + rot, axis=1).astype(BF16)


def _layout_w_ukv(w):
    hd = MLA_NOPE + MLA_V
    ks = [_place(w[:, h * hd:h * hd + MLA_NOPE], 0) for h in range(MLA_HEADS)]
    vs = [w[:, h * hd + MLA_NOPE:(h + 1) * hd] for h in range(MLA_HEADS)]
    return jnp.concatenate(ks + vs, axis=1).astype(BF16)


def _slope_columns(slopes2):
    p1 = slopes2.astype(BF16).astype(F32)
    p2 = (slopes2 - p1).astype(BF16).astype(F32)
    p3 = (slopes2 - p1 - p2).astype(BF16).astype(F32)
    pieces = jnp.stack([p1, p2, p3], axis=1)
    cols = jnp.concatenate([pieces * 65536.0, pieces * 256.0, pieces], axis=1)
    return _place(cols, POS_COL0).reshape(1, -1)


def _ordered_query_tiles(rel):
    B, S = rel.shape
    query_min = jnp.min(rel.reshape(B, S // DIFF_TQ, DIFF_TQ), axis=2)
    key_max = lax.cummax(jnp.max(rel.reshape(B, S // TK, TK), axis=2), axis=1)
    lowest = jnp.full((B, 1), jnp.iinfo(rel.dtype).min, rel.dtype)
    before = jnp.concatenate([lowest, key_max[:, :-1]], axis=1)
    before = jnp.repeat(before, TK // DIFF_TQ, axis=1)
    return (before <= query_min).astype(jnp.int32)


def _chunk_bias_diagonals(rel_bias):
    n_rel = rel_bias.shape[1]
    far = jnp.broadcast_to(rel_bias[:, n_rel - 1:], (rel_bias.shape[0], REL_MAX))
    wrap = far[:, :CHUNK]
    return jnp.concatenate([far, rel_bias[:, ::-1], wrap], axis=1) * LOG2E


def kernel(x, positions, attn_norm, w_in, mla_q_norm, mla_w_uq, mla_kv_norm, mla_w_ukv, diff_lambda, diff_norm, chunk_rel_bias, mla_out_norm, chunk_out_norm, w_out, ffn_norm, w_ffn_in, ffn_conv_w, ffn_conv_b, w_ffn_out, final_norm):
    B, S, D = x.shape
    pos_col = positions.astype(F32)[:, :, None]
    rel = positions - positions[:, :1]
    rel_f = rel.astype(F32)
    half = MLA_ROPE // 2
    inv = ROPE_THETA ** (-jnp.arange(half, dtype=F32) / half)
    inv_slot = _place(jnp.concatenate([inv, inv])[None, :], MLA_NOPE)
    slopes = 2.0 ** (-8.0 * jnp.arange(1, DIFF_HEADS + 1, dtype=F32) / DIFF_HEADS)
    slopes2 = slopes * LOG2E
    ordered = _ordered_query_tiles(rel)
    slope_cols = _slope_columns(slopes2)
    slopes2 = slopes2.reshape(DIFF_HEADS, 1, 1)
    row = lambda v: v.reshape(1, -1)

    for l in range(DEPTH):
        lam_init = 0.8 - 0.6 * math.exp(-0.3 * l)
        qm, km, vmt, qd, kd, vdt, qc, kc, vc = _inproj(
            x, pos_col, rel[:, :, None], row(attn_norm[l]), _layout_w_in(w_in[l]),
            row(mla_q_norm[l]), _layout_w_uq(mla_w_uq[l]), row(mla_kv_norm[l]),
            _layout_w_ukv(mla_w_ukv[l]), inv_slot, slope_cols)
        omt = _mla_attention(qm, km, vmt)
        odt = _diff_attention(ordered, qd, kd, vdt, rel_f[:, None, :], rel_f[:, :, None], slopes2,
                              diff_lambda[l], diff_norm[l].reshape(DIFF_HEADS, DIFF_V, 1), lam_init)
        pad = ((0, 0), (CH_PAD, 0), (0, 0))
        oc = _chunk_attention(qc, jnp.pad(kc, pad), jnp.pad(vc, pad),
                              _chunk_bias_diagonals(chunk_rel_bias[l]))
        x = _outproj_ffn(omt, odt, oc, x, mla_out_norm[l].reshape(-1, 1), row(chunk_out_norm[l]),
                         w_out[l].astype(BF16), row(ffn_norm[l]), w_ffn_in[l].astype(BF16),
                         ffn_conv_w[l], row(ffn_conv_b[l]), w_ffn_out[l].astype(BF16),
                         row(final_norm), final_norm=(l == DEPTH - 1))
    return x
```
